```python
import math
import jax, jax.numpy as jnp
from jax import lax
import numpy as np

D_MODEL = 1024
BATCH = 16
SEQ = 2048
DEPTH = 1
DEC_BATCH = 16
DEC_SEQ = 4096
PAST_LEN = 128

N_MEM = 256
S5_WIDTH = D_MODEL
S5_GROUP = 16
S5_GROUPS = S5_WIDTH // S5_GROUP
S5_STATE = 64
S5_CHUNK = 128
DT_MIN = 1e-3
DT_MAX = 1e-1
DIFF_HEADS = 8
DIFF_DH = D_MODEL // DIFF_HEADS // 2
DIFF_WIDTH = DIFF_HEADS * 2 * DIFF_DH
Q_BLOCK = 128
ROPE_THETA = 10000.0
MEM_HEADS = 4
MEM_DH = D_MODEL // MEM_HEADS
MEM_WIDTH = MEM_HEADS * MEM_DH
N_BRANCH = 3
D_FF = -(-8 * D_MODEL // (3 * 256)) * 256
IN_COLS = S5_WIDTH + 3 * DIFF_WIDTH + MEM_WIDTH + N_BRANCH * D_MODEL
SPLITS = [S5_WIDTH, S5_WIDTH + DIFF_WIDTH, S5_WIDTH + 2 * DIFF_WIDTH, S5_WIDTH + 3 * DIFF_WIDTH, S5_WIDTH + 3 * DIFF_WIDTH + MEM_WIDTH]
EPS = 1e-6

kernel_name = "hybrid_s5_diffattn_memxattn_encoder"

F32 = jnp.float32


def rmsnorm(x, g):
    xf = x.astype(F32)
    y = xf * lax.rsqrt(jnp.mean(xf * xf, axis=-1, keepdims=True) + EPS)
    return (y * g.astype(F32)).astype(x.dtype)


def rope(x):
    L, dh = x.shape[1], x.shape[-1]
    half = dh // 2
    inv = ROPE_THETA ** (-jnp.arange(half, dtype=F32) / half)
    ang = jnp.arange(L, dtype=F32)[:, None] * inv[None, :]
    cos = jnp.cos(ang)[None, :, None, :]
    sin = jnp.sin(ang)[None, :, None, :]
    xf = x.astype(F32)
    x1, x2 = xf[..., :half], xf[..., half:]
    return jnp.concatenate([x1 * cos - x2 * sin, x1 * sin + x2 * cos], axis=-1).astype(x.dtype)


def _lin_op(e1, e2):
    a1, b1 = e1
    a2, b2 = e2
    return a1 * a2, a2 * b1 + b2


def s5_direction(u, lam_re, lam_im, log_dt, b_re, b_im, c_re, c_im):
    Bsz, L, G, H = u.shape
    lam = lax.complex(lam_re.astype(F32), lam_im.astype(F32))
    dt = jnp.exp(log_dt.astype(F32))[:, None]
    lam_bar = jnp.exp(lam * dt)
    b_bar = ((lam_bar - 1.0) / lam)[..., None] * lax.complex(b_re.astype(F32), b_im.astype(F32))
    c = lax.complex(c_re.astype(F32), c_im.astype(F32))
    n_chunks = L // S5_CHUNK
    uc = u.reshape(Bsz, n_chunks, S5_CHUNK, G, H).transpose(1, 0, 2, 3, 4)

    def chunk_step(state, u_chunk):
        bu = jnp.einsum('bcgh,gph->bcgp', u_chunk.astype(jnp.complex64), b_bar)
        a = jnp.broadcast_to(lam_bar, bu.shape)
        a_cum, s = lax.associative_scan(_lin_op, (a, bu), axis=1)
        s = s + a_cum * state[:, None]
        y = jnp.einsum('bcgp,ghp->bcgh', s, c).real
        return s[:, -1], y

    state0 = jnp.zeros((Bsz, G, S5_STATE), jnp.complex64)
    _, ys = lax.scan(chunk_step, state0, uc)
    return ys.transpose(1, 0, 2, 3, 4).reshape(Bsz, L, G, H)


def diff_attention(q, k, v, q_g, k_g, lq1, lk1, lq2, lk2, sub_g, lambda_init):
    Bsz, L = q.shape[0], q.shape[1]
    q = rope(rmsnorm(q.reshape(Bsz, L, 2 * DIFF_HEADS, DIFF_DH), q_g))
    k = rope(rmsnorm(k.reshape(Bsz, L, 2 * DIFF_HEADS, DIFF_DH), k_g))
    v = v.reshape(Bsz, L, DIFF_HEADS, 2 * DIFF_DH)
    lam = (jnp.exp(jnp.sum(lq1.astype(F32) * lk1.astype(F32)))
           - jnp.exp(jnp.sum(lq2.astype(F32) * lk2.astype(F32))) + lambda_init)
    scale = DIFF_DH ** -0.5
    qb = q.reshape(Bsz, L // Q_BLOCK, Q_BLOCK, 2 * DIFF_HEADS, DIFF_DH).transpose(1, 0, 2, 3, 4)

    def block(q_blk):
        s = jnp.einsum('bqhd,bkhd->bhqk', q_blk, k, preferred_element_type=F32) * scale
        p = jax.nn.softmax(s, axis=-1).reshape(Bsz, DIFF_HEADS, 2, Q_BLOCK, L)
        a = p[:, :, 0] - lam * p[:, :, 1]
        return jnp.einsum('bhqk,bkhe->bqhe', a.astype(v.dtype), v)

    o = lax.map(block, qb)
    o = o.transpose(1, 0, 2, 3, 4).reshape(Bsz, L, DIFF_HEADS, 2 * DIFF_DH)
    o = rmsnorm(o, sub_g) * (1.0 - lambda_init)
    return o.reshape(Bsz, L, DIFF_WIDTH)


def memory_attention(q, mem_n, w_kv, q_g, k_g):
    Bsz, L = q.shape[0], q.shape[1]
    M = mem_n.shape[1]
    kv = mem_n @ w_kv
    k, v = jnp.split(kv, 2, axis=-1)
    q = rmsnorm(q.reshape(Bsz, L, MEM_HEADS, MEM_DH), q_g)
    k = rmsnorm(k.reshape(Bsz, M, MEM_HEADS, MEM_DH), k_g)
    v = v.reshape(Bsz, M, MEM_HEADS, MEM_DH)
    s = jnp.einsum('bqhd,bmhd->bhqm', q, k, preferred_element_type=F32) * (MEM_DH ** -0.5)
    p = jax.nn.softmax(s, axis=-1)
    o = jnp.einsum('bhqm,bmhe->bqhe', p.astype(v.dtype), v)
    return o.reshape(Bsz, L, MEM_WIDTH)


def layer(x, mem, li, norm_mix_g, w_in, b_gate,
          s5_lam_re, s5_lam_im, s5_log_dt, s5_b_re, s5_b_im, s5_c_re, s5_c_im, s5_d, s5_w_glu,
          diff_q_g, diff_k_g, diff_lq1, diff_lk1, diff_lq2, diff_lk2, diff_sub_g,
          mem_norm_g, w_mem_kv, mem_q_g, mem_k_g,
          w_branch, w_out, ffn_norm_g, w_gate_up, w_down):
    Bsz, L, _ = x.shape
    h = rmsnorm(x, norm_mix_g[li])
    proj = h @ w_in[li]
    u, q, k, v, qm, gl = jnp.split(proj, SPLITS, axis=-1)

    uf = u.astype(F32)
    ug = uf.reshape(Bsz, L, S5_GROUPS, S5_GROUP)
    y_f = s5_direction(ug, s5_lam_re[li, 0], s5_lam_im[li, 0], s5_log_dt[li, 0],
                       s5_b_re[li, 0], s5_b_im[li, 0], s5_c_re[li, 0], s5_c_im[li, 0])
    y_b = jnp.flip(s5_direction(jnp.flip(ug, axis=1), s5_lam_re[li, 1], s5_lam_im[li, 1], s5_log_dt[li, 1],
                                s5_b_re[li, 1], s5_b_im[li, 1], s5_c_re[li, 1], s5_c_im[li, 1]), axis=1)
    y = (y_f + y_b).reshape(Bsz, L, S5_WIDTH) + s5_d[li].astype(F32) * uf
    z = jax.nn.gelu(y)
    s5_out = (z * jax.nn.sigmoid(z @ s5_w_glu[li].astype(F32))).astype(x.dtype)

    lambda_init = 0.8 - 0.6 * math.exp(-0.3 * li)
    diff_out = diff_attention(q, k, v, diff_q_g[li], diff_k_g[li], diff_lq1[li], diff_lk1[li],
                              diff_lq2[li], diff_lk2[li], diff_sub_g[li], lambda_init)

    mem_out = memory_attention(qm, rmsnorm(mem, mem_norm_g[li]), w_mem_kv[li], mem_q_g[li], mem_k_g[li])

    branches = jnp.stack([s5_out, diff_out.astype(x.dtype), mem_out.astype(x.dtype)], axis=2)
    br = jnp.einsum('blnc,ncd->blnd', branches, w_branch[li])
    gates = jax.nn.sigmoid((gl + b_gate[li]).astype(F32)).reshape(Bsz, L, N_BRANCH, D_MODEL)
    merged = jnp.sum(gates * br.astype(F32), axis=2).astype(x.dtype)
    x = x + merged @ w_out[li]

    h2 = rmsnorm(x, ffn_norm_g[li])
    g, up = jnp.split(h2 @ w_gate_up[li], 2, axis=-1)
    x = x + (jax.nn.silu(g) * up) @ w_down[li]
    return x


def setup_inputs(seed: int = 0) -> dict:
    key = jax.random.key(seed)
    ks = jax.random.split(key, 40)
    nrm = lambda k, shape, s: jax.random.normal(k, shape, F32) * s
    gain = lambda k, shape: 1.0 + 0.02 * jax.random.normal(k, shape, F32)
    G, P, H = S5_GROUPS, S5_STATE, S5_GROUP
    n_idx = jnp.arange(P, dtype=F32)
    lam_re = -0.5 + 0.01 * jax.random.normal(ks[5], (DEPTH, 2, G, P), F32)
    lam_im = math.pi * n_idx + 0.01 * jax.random.normal(ks[6], (DEPTH, 2, G, P), F32)
    log_dt = jax.random.uniform(ks[7], (DEPTH, 2, G), F32, math.log(DT_MIN), math.log(DT_MAX))
    return {
        "x_prompt": nrm(ks[0], (BATCH, SEQ, D_MODEL), 1.0),
        "x_sample": nrm(ks[1], (DEC_BATCH, DEC_SEQ, D_MODEL), 1.0),
        "mem_prompt": nrm(ks[2], (BATCH, N_MEM, D_MODEL), 1.0),
        "mem_sample": nrm(ks[3], (DEC_BATCH, N_MEM, D_MODEL), 1.0),
        "norm_mix_g": gain(ks[4], (DEPTH, D_MODEL)),
        "w_in": nrm(ks[8], (DEPTH, D_MODEL, IN_COLS), D_MODEL ** -0.5),
        "b_gate": nrm(ks[9], (DEPTH, N_BRANCH * D_MODEL), 0.02),
        "s5_lam_re": lam_re,
        "s5_lam_im": lam_im,
        "s5_log_dt": log_dt,
        "s5_b_re": nrm(ks[10], (DEPTH, 2, G, P, H), (2 * H) ** -0.5),
        "s5_b_im": nrm(ks[11], (DEPTH, 2, G, P, H), (2 * H) ** -0.5),
        "s5_c_re": nrm(ks[12], (DEPTH, 2, G, H, P), (2 * P) ** -0.5),
        "s5_c_im": nrm(ks[13], (DEPTH, 2, G, H, P), (2 * P) ** -0.5),
        "s5_d": nrm(ks[14], (DEPTH, S5_WIDTH), 1.0),
        "s5_w_glu": nrm(ks[15], (DEPTH, S5_WIDTH, S5_WIDTH), S5_WIDTH ** -0.5),
        "diff_q_g": gain(ks[16], (DEPTH, DIFF_DH)),
        "diff_k_g": gain(ks[17], (DEPTH, DIFF_DH)),
        "diff_lq1": nrm(ks[18], (DEPTH, DIFF_DH), 0.1),
        "diff_lk1": nrm(ks[19], (DEPTH, DIFF_DH), 0.1),
        "diff_lq2": nrm(ks[20], (DEPTH, DIFF_DH), 0.1),
        "diff_lk2": nrm(ks[21], (DEPTH, DIFF_DH), 0.1),
        "diff_sub_g": gain(ks[22], (DEPTH, 2 * DIFF_DH)),
        "mem_norm_g": gain(ks[23], (DEPTH, D_MODEL)),
        "w_mem_kv": nrm(ks[24], (DEPTH, D_MODEL, 2 * MEM_WIDTH), D_MODEL ** -0.5),
        "mem_q_g": gain(ks[25], (DEPTH, MEM_DH)),
        "mem_k_g": gain(ks[26], (DEPTH, MEM_DH)),
        "w_branch": nrm(ks[27], (DEPTH, N_BRANCH, D_MODEL, D_MODEL), D_MODEL ** -0.5),
        "w_out": nrm(ks[28], (DEPTH, D_MODEL, D_MODEL), D_MODEL ** -0.5),
        "ffn_norm_g": gain(ks[29], (DEPTH, D_MODEL)),
        "w_gate_up": nrm(ks[30], (DEPTH, D_MODEL, 2 * D_FF), D_MODEL ** -0.5),
        "w_down": nrm(ks[31], (DEPTH, D_FF, D_MODEL), D_FF ** -0.5),
    }


def reference(x_prompt, x_sample, mem_prompt, mem_sample, norm_mix_g, w_in, b_gate,
              s5_lam_re, s5_lam_im, s5_log_dt, s5_b_re, s5_b_im, s5_c_re, s5_c_im, s5_d, s5_w_glu,
              diff_q_g, diff_k_g, diff_lq1, diff_lk1, diff_lq2, diff_lk2, diff_sub_g,
              mem_norm_g, w_mem_kv, mem_q_g, mem_k_g,
              w_branch, w_out, ffn_norm_g, w_gate_up, w_down):
    y_prompt = x_prompt
    y_sample = x_sample
    for li in range(DEPTH):
        y_prompt = layer(y_prompt, mem_prompt, li, norm_mix_g, w_in, b_gate,
                         s5_lam_re, s5_lam_im, s5_log_dt, s5_b_re, s5_b_im, s5_c_re, s5_c_im, s5_d, s5_w_glu,
                         diff_q_g, diff_k_g, diff_lq1, diff_lk1, diff_lq2, diff_lk2, diff_sub_g,
                         mem_norm_g, w_mem_kv, mem_q_g, mem_k_g,
                         w_branch, w_out, ffn_norm_g, w_gate_up, w_down)
        y_sample = layer(y_sample, mem_sample, li, norm_mix_g, w_in, b_gate,
                         s5_lam_re, s5_lam_im, s5_log_dt, s5_b_re, s5_b_im, s5_c_re, s5_c_im, s5_d, s5_w_glu,
                         diff_q_g, diff_k_g, diff_lq1, diff_lk1, diff_lq2, diff_lk2, diff_sub_g,
                         mem_norm_g, w_mem_kv, mem_q_g, mem_k_g,
                         w_branch, w_out, ffn_norm_g, w_gate_up, w_down)
    return (y_prompt, y_sample)
```

```python
import functools
import math

import jax
import jax.numpy as jnp
from jax import lax
from jax.experimental import pallas as pl
from jax.experimental.pallas import tpu as pltpu

F32 = jnp.float32
BF16 = jnp.bfloat16
EPS = 1e-6
ROPE_THETA = 10000.0

S5_GROUP = 16
S5_STATE = 64
CHUNK = 16
DIFF_HEADS = 8
DIFF_DH = 64
MEM_HEADS = 4
LANES = 128
VMEM_LIMIT = 56 * 1024 * 1024

HIGHEST = lax.Precision.HIGHEST


def _dot(a, b):
    return jnp.dot(a, b, preferred_element_type=F32)


def _dot_nt(a, b):
    return lax.dot_general(a, b, (((1,), (1,)), ((), ())), preferred_element_type=F32)


def _dot_hi(a, b):
    return jnp.dot(a, b, precision=HIGHEST, preferred_element_type=F32)


def _params(*sem):
    return pltpu.CompilerParams(dimension_semantics=sem, vmem_limit_bytes=VMEM_LIMIT)


def _resident(shape):
    nd = len(shape)
    return pl.BlockSpec(shape, lambda *_: (0,) * nd, pipeline_mode=pl.Buffered(1))


def _rope_kernel(cos_ref, sin_ref):
    rows = cos_ref.shape[0]
    base = pl.program_id(0) * rows
    pos = (lax.broadcasted_iota(jnp.int32, (rows, LANES), 0) + base).astype(F32)
    lane = lax.broadcasted_iota(jnp.int32, (rows, LANES), 1)
    half = DIFF_DH // 2
    freq = (lane % half).astype(F32)
    inv = jnp.exp(freq * (-math.log(ROPE_THETA) / half))
    ang = pos * inv
    cos_ref[...] = jnp.cos(ang)
    s = jnp.sin(ang)
    sin_ref[...] = jnp.where((lane % DIFF_DH) < half, -s, s)


def rope_tables(seq):
    rows = min(seq, 512)
    return pl.pallas_call(
        _rope_kernel,
        grid=(seq // rows,),
        out_specs=[pl.BlockSpec((rows, LANES), lambda i: (i, 0))] * 2,
        out_shape=[jax.ShapeDtypeStruct((seq, LANES), F32)] * 2,
        name="rope_tables",
    )()


def _proj_kernel(x_ref, g_ref, w_ref, bd_ref, bg_ref, qg_ref, kg_ref, mqg_ref, cos_ref, sin_ref,
                 u_ref, q_ref, k_ref, v_ref, qm_ref, gt_ref):
    d = x_ref.shape[-1]
    x = x_ref[0]
    hn = (x * lax.rsqrt(jnp.mean(x * x, axis=-1, keepdims=True) + EPS) * g_ref[...]).astype(BF16)

    def mm(j):
        return _dot(hn, w_ref[:, j * d:(j + 1) * d])

    u_ref[0] = mm(0).astype(BF16)

    cos = cos_ref[...]
    sin = sin_ref[...]
    lane = lax.broadcasted_iota(jnp.int32, cos.shape, 1)
    first_half = (lane % DIFF_DH) < (DIFF_DH // 2)

    def qk_norm_rope(t, gain, scale, out_ref):
        ms = _dot((t * t).astype(BF16), bd_ref[...])
        tn = t * lax.rsqrt(ms + EPS) * gain
        for j in range(d // LANES):
            blk = tn[:, j * LANES:(j + 1) * LANES]
            partner = jnp.where(first_half, pltpu.roll(blk, LANES - DIFF_DH // 2, 1), pltpu.roll(blk, DIFF_DH // 2, 1))
            out_ref[0, :, j * LANES:(j + 1) * LANES] = ((blk * cos + partner * sin) * scale).astype(BF16)

    qk_norm_rope(mm(1), qg_ref[...], DIFF_DH ** -0.5, q_ref)
    qk_norm_rope(mm(2), kg_ref[...], 1.0, k_ref)
    v_ref[0] = mm(3).astype(BF16)

    qm = mm(4)
    mdh = d // MEM_HEADS
    for h in range(MEM_HEADS):
        seg = qm[:, h * mdh:(h + 1) * mdh]
        segn = seg * lax.rsqrt(jnp.mean(seg * seg, axis=-1, keepdims=True) + EPS) * mqg_ref[:, h * mdh:(h + 1) * mdh]
        qm_ref[0, :, h * mdh:(h + 1) * mdh] = (segn * (mdh ** -0.5)).astype(BF16)

    for j in range(3):
        gl = mm(5 + j) + bg_ref[:, j * d:(j + 1) * d]
        gt_ref[0, :, j * d:(j + 1) * d] = jax.nn.sigmoid(gl).astype(BF16)


def proj(x, g, w_in, bd, b_gate, qg, kg, mqg, cos, sin, tt):
    b, seq, d = x.shape
    ncol = w_in.shape[1]
    tok = lambda width: pl.BlockSpec((1, tt, width), lambda i, j: (i, j, 0))
    outs = [d, d, d, d, d, 3 * d]
    return pl.pallas_call(
        _proj_kernel,
        grid=(b, seq // tt),
        in_specs=[tok(d), _resident((1, d)), _resident((d, ncol)), _resident((d, d)), _resident((1, 3 * d)),
                  _resident((1, d)), _resident((1, d)), _resident((1, d)),
                  pl.BlockSpec((tt, LANES), lambda i, j: (j, 0)), pl.BlockSpec((tt, LANES), lambda i, j: (j, 0))],
        out_specs=[tok(w) for w in outs],
        out_shape=[jax.ShapeDtypeStruct((b, seq, w), BF16) for w in outs],
        compiler_params=_params("parallel", "parallel"),
        name="proj",
    )(x, g, w_in, bd, b_gate, qg, kg, mqg, cos, sin)


def _s5_prep_kernel(lr_ref, li_ref, ldt_ref, btr_ref, bti_ref, lcr_ref, lci_ref, ldtc_ref, ctr_ref, cti_ref, d_ref,
                    w1_ref, w2_ref, a_ref):
    p, h, c = S5_STATE, S5_GROUP, CHUNK
    width = c * h
    lane128 = lax.broadcasted_iota(jnp.int32, (1, LANES), 1)
    fwd128 = lane128 < p

    dt = jnp.exp(ldt_ref[0])
    zr, zi = lr_ref[0] * dt, li_ref[0] * dt
    kk = jnp.minimum(lax.broadcasted_iota(jnp.int32, (24, 1), 0), c).astype(F32)
    mag = jnp.exp(kk * zr)
    pwr, pwi = mag * jnp.cos(kk * zi), mag * jnp.sin(kk * zi)

    lr, li = lr_ref[0], li_ref[0]
    nr, ni = pwr[1:2] - 1.0, pwi[1:2]
    den = lr * lr + li * li
    cr, ci = (nr * lr + ni * li) / den, (ni * lr - nr * li) / den
    btr, bti = btr_ref[0], bti_ref[0]
    bbr, bbi = cr * btr - ci * bti, cr * bti + ci * btr

    for s in range(c):
        sr = jnp.where(fwd128, pwr[c - 1 - s:c - s], pwr[s:s + 1])
        si = jnp.where(fwd128, pwi[c - 1 - s:c - s], pwi[s:s + 1])
        w1_ref[0, s * h:(s + 1) * h, width:width + LANES] = (bbr * sr - bbi * si).astype(BF16)
        w1_ref[0, s * h:(s + 1) * h, width + LANES:width + 2 * LANES] = (bbr * si + bbi * sr).astype(BF16)

    krow = jnp.minimum(lax.broadcasted_iota(jnp.int32, (1, LANES), 1), c).astype(F32)
    sel_k = lax.broadcasted_iota(jnp.int32, (LANES, width), 0)
    sel_t = lax.broadcasted_iota(jnp.int32, (LANES, width), 1) // h

    def expand(pw, power_of_t):
        return _dot_hi(pw, (sel_k == power_of_t(sel_t)).astype(F32))

    zr_stack, zi_stack = [], []
    for dr in range(2):
        dtc = jnp.exp(ldtc_ref[dr, 0])
        zrc, zic = lcr_ref[dr, 0] * dtc, lci_ref[dr, 0] * dtc
        magc = jnp.exp(zrc * krow)
        pcr, pci = magc * jnp.cos(zic * krow), magc * jnp.sin(zic * krow)
        ctr, cti = ctr_ref[dr, 0], cti_ref[dr, 0]
        lag = (lambda t: t) if dr == 0 else (lambda t: c - 1 - t)
        er, ei = expand(pcr, lag), expand(pci, lag)
        zr_stack.append(er * ctr - ei * cti)
        zi_stack.append(er * cti + ei * ctr)
        rd = (lambda t: t + 1) if dr == 0 else (lambda t: c - t)
        er, ei = expand(pcr, rd), expand(pci, rd)
        re_rows = (er * ctr - ei * cti).astype(BF16)
        im_rows = (-(er * cti + ei * ctr)).astype(BF16)
        zero = jnp.zeros((p, width), BF16)
        base = dr * 2 * LANES
        if dr == 0:
            w2_ref[0, base:base + LANES] = jnp.concatenate([re_rows, zero], axis=0)
            w2_ref[0, base + LANES:base + 2 * LANES] = jnp.concatenate([im_rows, zero], axis=0)
        else:
            w2_ref[0, base:base + LANES] = jnp.concatenate([zero, re_rows], axis=0)
            w2_ref[0, base + LANES:base + 2 * LANES] = jnp.concatenate([zero, im_rows], axis=0)

    zr_all = jnp.concatenate(zr_stack, axis=0)
    zi_all = jnp.concatenate(zi_stack, axis=0)
    lane16 = lax.broadcasted_iota(jnp.int32, (h, LANES), 1)
    g_f = _dot_hi(jnp.where(lane16 < p, bbr, 0.0), zr_all) - _dot_hi(jnp.where(lane16 < p, bbi, 0.0), zi_all)
    g_b = _dot_hi(jnp.where(lane16 < p, 0.0, bbr), zr_all) - _dot_hi(jnp.where(lane16 < p, 0.0, bbi), zi_all)

    lane = lax.broadcasted_iota(jnp.int32, (h, width), 1)
    row = lax.broadcasted_iota(jnp.int32, (h, width), 0)
    dcol = d_ref[0]
    for s in range(c):
        m_f = jnp.where(lane >= h * s, g_f if s == 0 else pltpu.roll(g_f, h * s, 1), 0.0)
        m_b = jnp.where(lane < h * (s + 1), g_b if s == c - 1 else pltpu.roll(g_b, h * (s + 1), 1), 0.0)
        skip = jnp.where(lane == h * s + row, dcol, 0.0)
        w1_ref[0, s * h:(s + 1) * h, 0:width] = (m_f + m_b + skip).astype(BF16)

    r8 = lax.broadcasted_iota(jnp.int32, (8, LANES), 0)
    a_ref[0] = jnp.where(r8 == 0, pwr[c:c + 1], jnp.where(r8 == 1, pwi[c:c + 1], 0.0))


def s5_prep(lam_re, lam_im, log_dt, b_re, b_im, c_re, c_im, d):
    _, g, p = lam_re.shape
    h = S5_GROUP
    width = CHUNK * h
    pack = lambda a: jnp.concatenate([a[0], a[1]], axis=-1)
    lr = pack(lam_re)[:, None, :]
    li = pack(lam_im)[:, None, :]
    ldt = pack(jnp.broadcast_to(log_dt[..., None], (2, g, p)))[:, None, :]
    btr = pack(jnp.swapaxes(b_re, -1, -2))
    bti = pack(jnp.swapaxes(b_im, -1, -2))
    ctr = jnp.tile(jnp.swapaxes(c_re, -1, -2), (1, 1, 1, CHUNK))
    cti = jnp.tile(jnp.swapaxes(c_im, -1, -2), (1, 1, 1, CHUNK))
    row = lambda last: pl.BlockSpec((1, 1, last), lambda i: (i, 0, 0))
    col = lambda rows, last: pl.BlockSpec((2, 1, rows, last), lambda i: (0, i, 0, 0))
    return pl.pallas_call(
        _s5_prep_kernel,
        grid=(g,),
        in_specs=[row(LANES), row(LANES), row(LANES),
                  pl.BlockSpec((1, h, LANES), lambda i: (i, 0, 0)), pl.BlockSpec((1, h, LANES), lambda i: (i, 0, 0)),
                  col(p, 1), col(p, 1), col(1, 1), col(p, width), col(p, width),
                  pl.BlockSpec((1, h, 1), lambda i: (i, 0, 0))],
        out_specs=[pl.BlockSpec((1, width, 2 * width), lambda i: (i, 0, 0)),
                   pl.BlockSpec((1, 2 * width, width), lambda i: (i, 0, 0)),
                   pl.BlockSpec((1, 8, LANES), lambda i: (i, 0, 0))],
        out_shape=[jax.ShapeDtypeStruct((g, width, 2 * width), BF16),
                   jax.ShapeDtypeStruct((g, 2 * width, width), BF16),
                   jax.ShapeDtypeStruct((g, 8, LANES), F32)],
        compiler_params=_params("parallel"),
        name="s5_prep",
    )(lr, li, ldt, btr, bti, lam_re[..., None], lam_im[..., None], log_dt[..., None, None], ctr, cti,
      d.reshape(g, h, 1))


def _s5_chunk_kernel(u_ref, w1_ref, w2_ref, a_ref, y_ref, yint_ref, sin_ref, st_ref, *, rows_per_chunk):
    gb, nrows, width = u_ref.shape
    rb = rows_per_chunk
    n_chunks = nrows // rb
    fwd = lax.broadcasted_iota(jnp.int32, (rb, LANES), 1) < S5_STATE

    for g in range(gb):
        r = _dot(u_ref[g], w1_ref[g])
        yint_ref[...] = r[:, :width]
        sin_ref[...] = r[:, width:]
        a_re = a_ref[g, 0:1, :]
        a_im = a_ref[g, 1:2, :]

        def step(i, carry):
            s_re, s_im = carry
            rf = pl.ds(pl.multiple_of(i * rb, rb), rb)
            rr = pl.ds(pl.multiple_of((n_chunks - 1 - i) * rb, rb), rb)
            st_ref[rf, 0:LANES] = s_re
            st_ref[rf, LANES:2 * LANES] = s_im
            st_ref[rr, 2 * LANES:3 * LANES] = s_re
            st_ref[rr, 3 * LANES:4 * LANES] = s_im
            in_re = jnp.where(fwd, sin_ref[rf, 0:LANES], sin_ref[rr, 0:LANES])
            in_im = jnp.where(fwd, sin_ref[rf, LANES:2 * LANES], sin_ref[rr, LANES:2 * LANES])
            return (a_re * s_re - a_im * s_im + in_re, a_re * s_im + a_im * s_re + in_im)

        zero = jnp.zeros((rb, LANES), F32)
        lax.fori_loop(0, n_chunks, step, (zero, zero), unroll=4)
        y_ref[g] = (yint_ref[...] + _dot(st_ref[...].astype(BF16), w2_ref[g])).astype(BF16)


def s5_chunk(u2, w1, w2, a, rows_per_chunk, gb):
    g, nrows, width = u2.shape
    kern = functools.partial(_s5_chunk_kernel, rows_per_chunk=rows_per_chunk)
    return pl.pallas_call(
        kern,
        grid=(g // gb,),
        in_specs=[pl.BlockSpec((gb, nrows, width), lambda i: (i, 0, 0)),
                  pl.BlockSpec((gb, width, 2 * width), lambda i: (i, 0, 0)),
                  pl.BlockSpec((gb, 2 * width, width), lambda i: (i, 0, 0)),
                  pl.BlockSpec((gb, 8, LANES), lambda i: (i, 0, 0))],
        out_specs=pl.BlockSpec((gb, nrows, width), lambda i: (i, 0, 0)),
        out_shape=jax.ShapeDtypeStruct((g, nrows, width), BF16),
        scratch_shapes=[pltpu.VMEM((nrows, width), F32), pltpu.VMEM((nrows, width), F32),
                        pltpu.VMEM((nrows, 2 * width), F32)],
        compiler_params=_params("parallel"),
        name="s5_chunk",
    )(u2, w1, w2, a)


def _diff_attn_kernel(q_ref, k_ref, v_ref, lq1_ref, lk1_ref, lq2_ref, lk2_ref, sg_ref, o_ref, *, lambda_init):
    q = q_ref[0]
    k = k_ref[0]
    dh = DIFF_DH
    lam = (jnp.exp(jnp.sum(lq1_ref[...] * lk1_ref[...], axis=-1, keepdims=True))
           - jnp.exp(jnp.sum(lq2_ref[...] * lk2_ref[...], axis=-1, keepdims=True)) + lambda_init)

    def softmax_parts(s):
        e = jnp.exp(s - jnp.max(s, axis=-1, keepdims=True))
        return e, 1.0 / jnp.sum(e, axis=-1, keepdims=True)

    e1, r1 = softmax_parts(_dot_nt(q[:, :dh], k[:, :dh]))
    e2, r2 = softmax_parts(_dot_nt(q[:, dh:], k[:, dh:]))
    a = e1 * r1 - e2 * (r2 * lam)
    o = _dot(a.astype(BF16), v_ref[0])
    on = o * lax.rsqrt(jnp.mean(o * o, axis=-1, keepdims=True) + EPS) * sg_ref[...]
    o_ref[0] = (on * (1.0 - lambda_init)).astype(BF16)


def diff_attn(q, k, v, lq1, lk1, lq2, lk2, sub_g, lambda_init, tq):
    b, seq, d = q.shape
    hw = 2 * DIFF_DH
    small = lambda n: pl.BlockSpec((1, n), lambda i, h, j: (0, 0))
    kern = functools.partial(_diff_attn_kernel, lambda_init=lambda_init)
    return pl.pallas_call(
        kern,
        grid=(b, DIFF_HEADS, seq // tq),
        in_specs=[pl.BlockSpec((1, tq, hw), lambda i, h, j: (i, j, h)),
                  pl.BlockSpec((1, seq, hw), lambda i, h, j: (i, 0, h)),
                  pl.BlockSpec((1, seq, hw), lambda i, h, j: (i, 0, h)),
                  small(DIFF_DH), small(DIFF_DH), small(DIFF_DH), small(DIFF_DH), small(hw)],
        out_specs=pl.BlockSpec((1, tq, hw), lambda i, h, j: (i, j, h)),
        out_shape=jax.ShapeDtypeStruct((b, seq, d), BF16),
        compiler_params=_params("parallel", "parallel", "parallel"),
        name="diff_attn",
    )(q, k, v, lq1, lk1, lq2, lk2, sub_g)


def _mem_kv_kernel(m_ref, g_ref, w_ref, kg_ref, k_ref, v_ref):
    d = m_ref.shape[-1]
    m = m_ref[0]
    mn = (m * lax.rsqrt(jnp.mean(m * m, axis=-1, keepdims=True) + EPS) * g_ref[...]).astype(BF16)
    kv = _dot(mn, w_ref[...])
    mdh = d // MEM_HEADS
    for h in range(MEM_HEADS):
        seg = kv[:, h * mdh:(h + 1) * mdh]
        segn = seg * lax.rsqrt(jnp.mean(seg * seg, axis=-1, keepdims=True) + EPS) * kg_ref[:, h * mdh:(h + 1) * mdh]
        k_ref[0, :, h * mdh:(h + 1) * mdh] = segn.astype(BF16)
    v_ref[0] = kv[:, d:].astype(BF16)


def mem_kv(mem, g, w_kv, kg):
    b, m, d = mem.shape
    blk = pl.BlockSpec((1, m, d), lambda i: (i, 0, 0))
    return pl.pallas_call(
        _mem_kv_kernel,
        grid=(b,),
        in_specs=[blk, _resident((1, d)), _resident((d, 2 * d)), _resident((1, d))],
        out_specs=[blk, blk],
        out_shape=[jax.ShapeDtypeStruct((b, m, d), BF16)] * 2,
        compiler_params=_params("parallel"),
        name="mem_kv",
    )(mem, g, w_kv, kg)


def _merge_kernel(x_ref, y_ref, do_ref, qm_ref, gt_ref, mk_ref, mv_ref, wglu_ref, wb_ref, wo_ref, o_ref):
    d = x_ref.shape[-1]
    z = jax.nn.gelu(y_ref[0].astype(F32))
    s5o = (z * jax.nn.sigmoid(_dot(z.astype(BF16), wglu_ref[...]))).astype(BF16)

    qm = qm_ref[0]
    mk = mk_ref[0]
    mv = mv_ref[0]
    mdh = d // MEM_HEADS
    mem_parts = []
    for h in range(MEM_HEADS):
        sl = slice(h * mdh, (h + 1) * mdh)
        s = _dot_nt(qm[:, sl], mk[:, sl])
        e = jnp.exp(s - jnp.max(s, axis=-1, keepdims=True))
        pr = e * (1.0 / jnp.sum(e, axis=-1, keepdims=True))
        mem_parts.append(_dot(pr.astype(BF16), mv[:, sl]).astype(BF16))
    mo = jnp.concatenate(mem_parts, axis=-1)

    merged = None
    for n, br in enumerate((s5o, do_ref[0], mo)):
        term = gt_ref[0, :, n * d:(n + 1) * d].astype(F32) * _dot(br, wb_ref[n])
        merged = term if merged is None else merged + term
    o_ref[0] = x_ref[0] + _dot(merged.astype(BF16), wo_ref[...])


def merge(x, y, do, qm, gates, mk, mv, w_glu, w_branch, w_out, tt):
    b, seq, d = x.shape
    m = mk.shape[1]
    tok = lambda width: pl.BlockSpec((1, tt, width), lambda i, j: (i, j, 0))
    memb = pl.BlockSpec((1, m, d), lambda i, j: (i, 0, 0))
    return pl.pallas_call(
        _merge_kernel,
        grid=(b, seq // tt),
        in_specs=[tok(d), tok(d), tok(d), tok(d), tok(3 * d), memb, memb,
                  _resident((d, d)), _resident((3, d, d)), _resident((d, d))],
        out_specs=tok(d),
        out_shape=jax.ShapeDtypeStruct((b, seq, d), F32),
        compiler_params=_params("parallel", "parallel"),
        name="merge",
    )(x, y, do, qm, gates, mk, mv, w_glu, w_branch, w_out)


def _ffn_kernel(x_ref, g_ref, wgu_ref, wd_ref, o_ref):
    x = x_ref[0]
    dff = wd_ref.shape[0]
    hn = (x * lax.rsqrt(jnp.mean(x * x, axis=-1, keepdims=True) + EPS) * g_ref[...]).astype(BF16)
    gate = _dot(hn, wgu_ref[:, :dff])
    up = _dot(hn, wgu_ref[:, dff:])
    act = (jax.nn.silu(gate) * up).astype(BF16)
    o_ref[0] = x + _dot(act, wd_ref[...])


def ffn(x, g, w_gate_up, w_down, tt):
    b, seq, d = x.shape
    dff = w_down.shape[0]
    tok = pl.BlockSpec((1, tt, d), lambda i, j: (i, j, 0))
    return pl.pallas_call(
        _ffn_kernel,
        grid=(b, seq // tt),
        in_specs=[tok, _resident((1, d)), _resident((d, 2 * dff)), _resident((dff, d))],
        out_specs=tok,
        out_shape=jax.ShapeDtypeStruct((b, seq, d), F32),
        compiler_params=_params("parallel", "parallel"),
        name="ffn",
    )(x, g, w_gate_up, w_down)


def _to_chunk_layout(u):
    b, seq, d = u.shape
    g = d // S5_GROUP
    u5 = u.reshape(b, seq // CHUNK, CHUNK, g, S5_GROUP)
    return u5.transpose(3, 1, 0, 2, 4).reshape(g, (seq // CHUNK) * b, CHUNK * S5_GROUP)


def _from_chunk_layout(y2, b):
    g, nrows, width = y2.shape
    nc = nrows // b
    y5 = y2.reshape(g, nc, b, CHUNK, S5_GROUP)
    return y5.transpose(2, 1, 3, 0, 4).reshape(b, nc * CHUNK, g * S5_GROUP)


def _layer(x, mem, p, li, s5_ops, tables):
    b, seq, d = x.shape
    tt = min(256, seq)
    row = lambda a: a.reshape(1, -1)
    tile_to_d = lambda a: jnp.tile(a, d // a.shape[0]).reshape(1, d)
    heads16 = lax.broadcasted_iota(jnp.int32, (d, d), 0) // DIFF_DH == lax.broadcasted_iota(jnp.int32, (d, d), 1) // DIFF_DH
    bd = jnp.where(heads16, 1.0 / DIFF_DH, 0.0).astype(BF16)
    cos, sin = tables

    u, q, k, v, qm, gates = proj(x, row(p["norm_mix_g"][li]), p["w_in"][li].astype(BF16), bd, row(p["b_gate"][li]),
                                 tile_to_d(p["diff_q_g"][li]), tile_to_d(p["diff_k_g"][li]), tile_to_d(p["mem_q_g"][li]),
                                 cos[:seq], sin[:seq], tt)

    w1, w2, a = s5_ops
    y2 = s5_chunk(_to_chunk_layout(u), w1, w2, a, b, 2)
    y = _from_chunk_layout(y2, b)

    lambda_init = 0.8 - 0.6 * math.exp(-0.3 * li)
    do = diff_attn(q, k, v, row(p["diff_lq1"][li]), row(p["diff_lk1"][li]), row(p["diff_lq2"][li]),
                   row(p["diff_lk2"][li]), row(p["diff_sub_g"][li]), lambda_init, tt)

    mk, mv = mem_kv(mem, row(p["mem_norm_g"][li]), p["w_mem_kv"][li].astype(BF16), tile_to_d(p["mem_k_g"][li]))
    x1 = merge(x, y, do, qm, gates, mk, mv, p["s5_w_glu"][li].astype(BF16), p["w_branch"][li].astype(BF16),
               p["w_out"][li].astype(BF16), tt)
    return ffn(x1, row(p["ffn_norm_g"][li]), p["w_gate_up"][li].astype(BF16), p["w_down"][li].astype(BF16), tt)


def kernel(x_prompt, x_sample, mem_prompt, mem_sample, norm_mix_g, w_in, b_gate, s5_lam_re, s5_lam_im, s5_log_dt, s5_b_re, s5_b_im, s5_c_re, s5_c_im, s5_d, s5_w_glu, diff_q_g, diff_k_g, diff_lq1, diff_lk1, diff_lq2, diff_lk2, diff_sub_g, mem_norm_g, w_mem_kv, mem_q_g, mem_k_g, w_branch, w_out, ffn_norm_g, w_gate_up, w_down):
    p = dict(norm_mix_g=norm_mix_g, w_in=w_in, b_gate=b_gate, s5_w_glu=s5_w_glu, diff_q_g=diff_q_g, diff_k_g=diff_k_g,
             diff_lq1=diff_lq1, diff_lk1=diff_lk1, diff_lq2=diff_lq2, diff_lk2=diff_lk2, diff_sub_g=diff_sub_g,
             mem_norm_g=mem_norm_g, w_mem_kv=w_mem_kv, mem_q_g=mem_q_g, mem_k_g=mem_k_g, w_branch=w_branch,
             w_out=w_out, ffn_norm_g=ffn_norm_g, w_gate_up=w_gate_up, w_down=w_down)
    tables = rope_tables(max(x_prompt.shape[1], x_sample.shape[1]))
    y_prompt, y_sample = x_prompt, x_sample
    for li in range(norm_mix_g.shape[0]):
        s5_ops = s5_prep(s5_lam_re[li], s5_lam_im[li], s5_log_dt[li], s5_b_re[li], s5_b_im[li],
                         s5_c_re[li], s5_c_im[li], s5_d[li])
        y_prompt = _layer(y_prompt, mem_prompt, p, li, s5_ops, tables)
        y_sample = _layer(y_sample, mem_sample, p, li, s5_ops, tables)
    return (y_prompt, y_sample)
```

```python
import functools
import math

import numpy as np
import jax
import jax.numpy as jnp
from jax import lax
from jax.experimental import pallas as pl
from jax.experimental.pallas import tpu as pltpu

F32 = jnp.float32
BF16 = jnp.bfloat16
EPS = 1e-6
ROPE_THETA = 10000.0
LOG2E = math.log2(math.e)

S5_GROUP = 16
S5_STATE = 64
CHUNK = 16
DIFF_HEADS = 8
DIFF_DH = 64
MEM_HEADS = 4
LANES = 128
GROUPS_PER_LANE_BLOCK = LANES // S5_GROUP
VMEM_LIMIT = 56 * 1024 * 1024
SCORE_BOUND = 60.0

HIGHEST = lax.Precision.HIGHEST


def _dot(a, b):
    return jnp.dot(a, b, preferred_element_type=F32)


def _dot_nt(a, b):
    return lax.dot_general(a, b, (((1,), (1,)), ((), ())), preferred_element_type=F32)


def _dot_hi(a, b):
    return jnp.dot(a, b, precision=HIGHEST, preferred_element_type=F32)


def _params(*sem):
    return pltpu.CompilerParams(dimension_semantics=sem, vmem_limit_bytes=VMEM_LIMIT)


def _resident(shape):
    nd = len(shape)
    return pl.BlockSpec(shape, lambda *_: (0,) * nd, pipeline_mode=pl.Buffered(1))


def _rope_kernel(cos_ref, sin_ref):
    rows = cos_ref.shape[0]
    base = pl.program_id(0) * rows
    pos = (lax.broadcasted_iota(jnp.int32, (rows, LANES), 0) + base).astype(F32)
    lane = lax.broadcasted_iota(jnp.int32, (rows, LANES), 1)
    half = DIFF_DH // 2
    freq = (lane % half).astype(F32)
    inv = jnp.exp(freq * (-math.log(ROPE_THETA) / half))
    ang = pos * inv
    cos_ref[...] = jnp.cos(ang)
    s = jnp.sin(ang)
    sin_ref[...] = jnp.where((lane % DIFF_DH) < half, -s, s)


def rope_tables(seq):
    rows = min(seq, 512)
    return pl.pallas_call(
        _rope_kernel,
        grid=(seq // rows,),
        out_specs=[pl.BlockSpec((rows, LANES), lambda i: (i, 0))] * 2,
        out_shape=[jax.ShapeDtypeStruct((seq, LANES), F32)] * 2,
        name="rope_tables",
    )()


def _proj_kernel(x_ref, g_ref, w_ref, bd_ref, bg_ref, qg_ref, kg_ref, mqg_ref, cos_ref, sin_ref,
                 u_ref, q_ref, k_ref, v_ref, qm_ref, gt_ref):
    d = x_ref.shape[-1]
    x = x_ref[0]
    hn = (x * lax.rsqrt(jnp.mean(x * x, axis=-1, keepdims=True) + EPS) * g_ref[...]).astype(BF16)

    def mm(j):
        return _dot(hn, w_ref[:, j * d:(j + 1) * d])

    u_ref[0] = mm(0).astype(BF16)

    cos = cos_ref[...]
    sin = sin_ref[...]
    lane = lax.broadcasted_iota(jnp.int32, cos.shape, 1)
    first_half = (lane % DIFF_DH) < (DIFF_DH // 2)

    def qk_norm_rope(t, gain, scale, out_ref):
        ms = _dot((t * t).astype(BF16), bd_ref[...])
        tn = t * lax.rsqrt(ms + EPS) * gain
        for j in range(d // LANES):
            blk = tn[:, j * LANES:(j + 1) * LANES]
            partner = jnp.where(first_half, pltpu.roll(blk, LANES - DIFF_DH // 2, 1), pltpu.roll(blk, DIFF_DH // 2, 1))
            out_ref[0, :, j * LANES:(j + 1) * LANES] = ((blk * cos + partner * sin) * scale).astype(BF16)

    qk_norm_rope(mm(1), qg_ref[...], DIFF_DH ** -0.5 * LOG2E, q_ref)
    qk_norm_rope(mm(2), kg_ref[...], 1.0, k_ref)
    v_ref[0] = mm(3).astype(BF16)

    qm = mm(4)
    mdh = d // MEM_HEADS
    for h in range(MEM_HEADS):
        seg = qm[:, h * mdh:(h + 1) * mdh]
        segn = seg * lax.rsqrt(jnp.mean(seg * seg, axis=-1, keepdims=True) + EPS) * mqg_ref[:, h * mdh:(h + 1) * mdh]
        qm_ref[0, :, h * mdh:(h + 1) * mdh] = (segn * (mdh ** -0.5)).astype(BF16)

    for j in range(3):
        gl = mm(5 + j) + bg_ref[:, j * d:(j + 1) * d]
        gt_ref[0, :, j * d:(j + 1) * d] = jax.nn.sigmoid(gl).astype(BF16)


def proj(x, g, w_in, bd, b_gate, qg, kg, mqg, cos, sin, tt):
    b, seq, d = x.shape
    ncol = w_in.shape[1]
    tok = lambda width: pl.BlockSpec((1, tt, width), lambda i, j: (i, j, 0))
    outs = [d, d, d, d, d, 3 * d]
    return pl.pallas_call(
        _proj_kernel,
        grid=(b, seq // tt),
        in_specs=[tok(d), _resident((1, d)), _resident((d, ncol)), _resident((d, d)), _resident((1, 3 * d)),
                  _resident((1, d)), _resident((1, d)), _resident((1, d)),
                  pl.BlockSpec((tt, LANES), lambda i, j: (j, 0)), pl.BlockSpec((tt, LANES), lambda i, j: (j, 0))],
        out_specs=[tok(w) for w in outs],
        out_shape=[jax.ShapeDtypeStruct((b, seq, w), BF16) for w in outs],
        compiler_params=_params("parallel", "parallel"),
        name="proj",
    )(x, g, w_in, bd, b_gate, qg, kg, mqg, cos, sin)


def _s5_prep_kernel(lr_ref, li_ref, ldt_ref, btr_ref, bti_ref, lcr_ref, lci_ref, ldtc_ref, ctr_ref, cti_ref, d_ref,
                    w1_ref, w2_ref, a_ref):
    p, h, c = S5_STATE, S5_GROUP, CHUNK
    width = c * h
    lane128 = lax.broadcasted_iota(jnp.int32, (1, LANES), 1)
    fwd128 = lane128 < p

    dt = jnp.exp(ldt_ref[0])
    zr, zi = lr_ref[0] * dt, li_ref[0] * dt
    kk = jnp.minimum(lax.broadcasted_iota(jnp.int32, (24, 1), 0), c).astype(F32)
    mag = jnp.exp(kk * zr)
    pwr, pwi = mag * jnp.cos(kk * zi), mag * jnp.sin(kk * zi)

    lr, li = lr_ref[0], li_ref[0]
    nr, ni = pwr[1:2] - 1.0, pwi[1:2]
    den = lr * lr + li * li
    cr, ci = (nr * lr + ni * li) / den, (ni * lr - nr * li) / den
    btr, bti = btr_ref[0], bti_ref[0]
    bbr, bbi = cr * btr - ci * bti, cr * bti + ci * btr

    for s in range(c):
        sr = jnp.where(fwd128, pwr[c - 1 - s:c - s], pwr[s:s + 1])
        si = jnp.where(fwd128, pwi[c - 1 - s:c - s], pwi[s:s + 1])
        w1_ref[0, s * h:(s + 1) * h, width:width + LANES] = (bbr * sr - bbi * si).astype(BF16)
        w1_ref[0, s * h:(s + 1) * h, width + LANES:width + 2 * LANES] = (bbr * si + bbi * sr).astype(BF16)

    krow = jnp.minimum(lax.broadcasted_iota(jnp.int32, (1, LANES), 1), c).astype(F32)
    sel_k = lax.broadcasted_iota(jnp.int32, (LANES, width), 0)
    sel_t = lax.broadcasted_iota(jnp.int32, (LANES, width), 1) // h

    def expand(pw, power_of_t):
        return _dot_hi(pw, (sel_k == power_of_t(sel_t)).astype(F32))

    zr_stack, zi_stack = [], []
    for dr in range(2):
        dtc = jnp.exp(ldtc_ref[dr, 0])
        zrc, zic = lcr_ref[dr, 0] * dtc, lci_ref[dr, 0] * dtc
        magc = jnp.exp(zrc * krow)
        pcr, pci = magc * jnp.cos(zic * krow), magc * jnp.sin(zic * krow)
        ctr, cti = ctr_ref[dr, 0], cti_ref[dr, 0]
        lag = (lambda t: t) if dr == 0 else (lambda t: c - 1 - t)
        er, ei = expand(pcr, lag), expand(pci, lag)
        zr_stack.append(er * ctr - ei * cti)
        zi_stack.append(er * cti + ei * ctr)
        rd = (lambda t: t + 1) if dr == 0 else (lambda t: c - t)
        er, ei = expand(pcr, rd), expand(pci, rd)
        re_rows = (er * ctr - ei * cti).astype(BF16)
        im_rows = (-(er * cti + ei * ctr)).astype(BF16)
        zero = jnp.zeros((p, width), BF16)
        base = dr * 2 * LANES
        if dr == 0:
            w2_ref[0, base:base + LANES] = jnp.concatenate([re_rows, zero], axis=0)
            w2_ref[0, base + LANES:base + 2 * LANES] = jnp.concatenate([im_rows, zero], axis=0)
        else:
            w2_ref[0, base:base + LANES] = jnp.concatenate([zero, re_rows], axis=0)
            w2_ref[0, base + LANES:base + 2 * LANES] = jnp.concatenate([zero, im_rows], axis=0)

    zr_all = jnp.concatenate(zr_stack, axis=0)
    zi_all = jnp.concatenate(zi_stack, axis=0)
    lane16 = lax.broadcasted_iota(jnp.int32, (h, LANES), 1)
    g_f = _dot_hi(jnp.where(lane16 < p, bbr, 0.0), zr_all) - _dot_hi(jnp.where(lane16 < p, bbi, 0.0), zi_all)
    g_b = _dot_hi(jnp.where(lane16 < p, 0.0, bbr), zr_all) - _dot_hi(jnp.where(lane16 < p, 0.0, bbi), zi_all)

    lane = lax.broadcasted_iota(jnp.int32, (h, width), 1)
    row = lax.broadcasted_iota(jnp.int32, (h, width), 0)
    dcol = d_ref[0]
    for s in range(c):
        m_f = jnp.where(lane >= h * s, g_f if s == 0 else pltpu.roll(g_f, h * s, 1), 0.0)
        m_b = jnp.where(lane < h * (s + 1), g_b if s == c - 1 else pltpu.roll(g_b, h * (s + 1), 1), 0.0)
        skip = jnp.where(lane == h * s + row, dcol, 0.0)
        w1_ref[0, s * h:(s + 1) * h, 0:width] = (m_f + m_b + skip).astype(BF16)

    r8 = lax.broadcasted_iota(jnp.int32, (8, LANES), 0)
    a_ref[0] = jnp.where(r8 == 0, pwr[c:c + 1], jnp.where(r8 == 1, pwi[c:c + 1], 0.0))


def s5_prep(lam_re, lam_im, log_dt, b_re, b_im, c_re, c_im, d):
    _, g, p = lam_re.shape
    h = S5_GROUP
    width = CHUNK * h
    pack = lambda a: jnp.concatenate([a[0], a[1]], axis=-1)
    lr = pack(lam_re)[:, None, :]
    li = pack(lam_im)[:, None, :]
    ldt = pack(jnp.broadcast_to(log_dt[..., None], (2, g, p)))[:, None, :]
    btr = pack(jnp.swapaxes(b_re, -1, -2))
    bti = pack(jnp.swapaxes(b_im, -1, -2))
    ctr = jnp.tile(jnp.swapaxes(c_re, -1, -2), (1, 1, 1, CHUNK))
    cti = jnp.tile(jnp.swapaxes(c_im, -1, -2), (1, 1, 1, CHUNK))
    row = lambda last: pl.BlockSpec((1, 1, last), lambda i: (i, 0, 0))
    col = lambda rows, last: pl.BlockSpec((2, 1, rows, last), lambda i: (0, i, 0, 0))
    return pl.pallas_call(
        _s5_prep_kernel,
        grid=(g,),
        in_specs=[row(LANES), row(LANES), row(LANES),
                  pl.BlockSpec((1, h, LANES), lambda i: (i, 0, 0)), pl.BlockSpec((1, h, LANES), lambda i: (i, 0, 0)),
                  col(p, 1), col(p, 1), col(1, 1), col(p, width), col(p, width),
                  pl.BlockSpec((1, h, 1), lambda i: (i, 0, 0))],
        out_specs=[pl.BlockSpec((1, width, 2 * width), lambda i: (i, 0, 0)),
                   pl.BlockSpec((1, 2 * width, width), lambda i: (i, 0, 0)),
                   pl.BlockSpec((1, 8, LANES), lambda i: (i, 0, 0))],
        out_shape=[jax.ShapeDtypeStruct((g, width, 2 * width), BF16),
                   jax.ShapeDtypeStruct((g, 2 * width, width), BF16),
                   jax.ShapeDtypeStruct((g, 8, LANES), F32)],
        compiler_params=_params("parallel"),
        name="s5_prep",
    )(lr, li, ldt, btr, bti, lam_re[..., None], lam_im[..., None], log_dt[..., None, None], ctr, cti,
      d.reshape(g, h, 1))


def _chunk_permutations():
    n = GROUPS_PER_LANE_BLOCK
    pm = np.zeros((n, CHUNK // 2, 2 * LANES, CHUNK * S5_GROUP), np.float32)
    for g in range(n):
        for tp in range(CHUNK // 2):
            for j in range(2):
                for h in range(S5_GROUP):
                    pm[g, tp, j * LANES + g * S5_GROUP + h, (2 * tp + j) * S5_GROUP + h] = 1.0
    return pm


def _to_chunks_kernel(x_ref, p_ref, o_ref):
    cb, _, bsz, _ = x_ref.shape
    rows = cb * bsz
    pairs = [jnp.concatenate([x_ref[:, 2 * tp].reshape(rows, LANES), x_ref[:, 2 * tp + 1].reshape(rows, LANES)], axis=1)
             for tp in range(CHUNK // 2)]
    for g in range(GROUPS_PER_LANE_BLOCK):
        acc = _dot(pairs[0], p_ref[g, 0])
        for tp in range(1, CHUNK // 2):
            acc = acc + _dot(pairs[tp], p_ref[g, tp])
        o_ref[g] = acc.astype(BF16)


def _from_chunks_kernel(y_ref, p_ref, o_ref):
    cb, _, bsz, _ = o_ref.shape
    for tp in range(CHUNK // 2):
        acc = _dot_nt(y_ref[0], p_ref[0, tp])
        for g in range(1, GROUPS_PER_LANE_BLOCK):
            acc = acc + _dot_nt(y_ref[g], p_ref[g, tp])
        o_ref[:, 2 * tp] = acc[:, :LANES].astype(BF16).reshape(cb, bsz, LANES)
        o_ref[:, 2 * tp + 1] = acc[:, LANES:].astype(BF16).reshape(cb, bsz, LANES)


def _chunk_relayout(kern, x, perm, out_struct, cb, to_chunks):
    n = GROUPS_PER_LANE_BLOCK
    if to_chunks:
        nc, _, bsz, d = x.shape
    else:
        nc, _, bsz, d = out_struct.shape
    tm_spec = pl.BlockSpec((cb, CHUNK, bsz, LANES), lambda i, j: (i, 0, 0, j))
    ch_spec = pl.BlockSpec((n, cb * bsz, CHUNK * S5_GROUP), lambda i, j: (j, i, 0))
    return pl.pallas_call(
        kern,
        grid=(nc // cb, d // LANES),
        in_specs=[tm_spec if to_chunks else ch_spec, _resident(perm.shape)],
        out_specs=ch_spec if to_chunks else tm_spec,
        out_shape=out_struct,
        compiler_params=_params("parallel", "parallel"),
        name="to_chunks" if to_chunks else "from_chunks",
    )(x, perm)


def _s5_chunk_kernel(u_ref, w1_ref, w2_ref, a_ref, y_ref, yint_ref, sin_ref, st_ref, *, rows_per_chunk):
    gb, nrows, width = u_ref.shape
    rb = rows_per_chunk
    n_chunks = nrows // rb
    fwd = lax.broadcasted_iota(jnp.int32, (rb, LANES), 1) < S5_STATE

    for g in range(gb):
        r = _dot(u_ref[g], w1_ref[g])
        yint_ref[...] = r[:, :width]
        sin_ref[...] = r[:, width:]
        a_re = a_ref[g, 0:1, :]
        a_im = a_ref[g, 1:2, :]

        def step(i, carry):
            s_re, s_im = carry
            rf = pl.ds(pl.multiple_of(i * rb, rb), rb)
            rr = pl.ds(pl.multiple_of((n_chunks - 1 - i) * rb, rb), rb)
            st_ref[rf, 0:LANES] = s_re
            st_ref[rf, LANES:2 * LANES] = s_im
            st_ref[rr, 2 * LANES:3 * LANES] = s_re
            st_ref[rr, 3 * LANES:4 * LANES] = s_im
            in_re = jnp.where(fwd, sin_ref[rf, 0:LANES], sin_ref[rr, 0:LANES])
            in_im = jnp.where(fwd, sin_ref[rf, LANES:2 * LANES], sin_ref[rr, LANES:2 * LANES])
            return (a_re * s_re - a_im * s_im + in_re, a_re * s_im + a_im * s_re + in_im)

        zero = jnp.zeros((rb, LANES), F32)
        lax.fori_loop(0, n_chunks, step, (zero, zero), unroll=4)
        y_ref[g] = (yint_ref[...] + _dot(st_ref[...].astype(BF16), w2_ref[g])).astype(BF16)


def s5_chunk(u2, w1, w2, a, rows_per_chunk, gb):
    g, nrows, width = u2.shape
    kern = functools.partial(_s5_chunk_kernel, rows_per_chunk=rows_per_chunk)
    return pl.pallas_call(
        kern,
        grid=(g // gb,),
        in_specs=[pl.BlockSpec((gb, nrows, width), lambda i: (i, 0, 0)),
                  pl.BlockSpec((gb, width, 2 * width), lambda i: (i, 0, 0)),
                  pl.BlockSpec((gb, 2 * width, width), lambda i: (i, 0, 0)),
                  pl.BlockSpec((gb, 8, LANES), lambda i: (i, 0, 0))],
        out_specs=pl.BlockSpec((gb, nrows, width), lambda i: (i, 0, 0)),
        out_shape=jax.ShapeDtypeStruct((g, nrows, width), BF16),
        scratch_shapes=[pltpu.VMEM((nrows, width), F32), pltpu.VMEM((nrows, width), F32),
                        pltpu.VMEM((nrows, 2 * width), F32)],
        compiler_params=_params("parallel"),
        name="s5_chunk",
    )(u2, w1, w2, a)


def _diff_lambda(lq1_ref, lk1_ref, lq2_ref, lk2_ref, lambda_init):
    return (jnp.exp(jnp.sum(lq1_ref[...] * lk1_ref[...], axis=-1, keepdims=True))
            - jnp.exp(jnp.sum(lq2_ref[...] * lk2_ref[...], axis=-1, keepdims=True)) + lambda_init)


def _map_queries(q):
    lane = lax.broadcasted_iota(jnp.int32, q.shape, 1)
    zero = jnp.zeros_like(q)
    return jnp.where(lane < DIFF_DH, q, zero), jnp.where(lane < DIFF_DH, zero, q)


def _sublayer_norm_store(o, sg_ref, o_ref, lambda_init):
    on = o * lax.rsqrt(jnp.mean(o * o, axis=-1, keepdims=True) + EPS) * sg_ref[...]
    o_ref[0] = (on * (1.0 - lambda_init)).astype(BF16)


def _diff_attn_stream_kernel(q_ref, k_ref, v_ref, lq1_ref, lk1_ref, lq2_ref, lk2_ref, sg_ref, o_ref, *,
                             lambda_init, tk, lookahead):
    seq = k_ref.shape[1]
    dv = v_ref.shape[2]
    lam = _diff_lambda(lq1_ref, lk1_ref, lq2_ref, lk2_ref, lambda_init)
    qmaps = _map_queries(q_ref[0])
    ones = jnp.ones((tk, dv), BF16)

    def scores(step):
        mp, kb = step
        return _dot_nt(qmaps[mp], k_ref[0, kb * tk:(kb + 1) * tk, :])

    steps = [(mp, kb) for kb in range(seq // tk) for mp in (0, 1)]
    pending = [scores(st) for st in steps[:lookahead]]
    acc = [None, None]
    for idx, (mp, kb) in enumerate(steps):
        s = pending.pop(0)
        if idx + lookahead < len(steps):
            pending.append(scores(steps[idx + lookahead]))
        p = jnp.exp2(s).astype(BF16)
        v_aug = jnp.concatenate([v_ref[0, kb * tk:(kb + 1) * tk, :], ones], axis=1)
        term = _dot(p, v_aug)
        acc[mp] = term if acc[mp] is None else acc[mp] + term
    o1 = acc[0][:, :dv] * (1.0 / acc[0][:, dv:])
    o2 = acc[1][:, :dv] * (1.0 / acc[1][:, dv:])
    _sublayer_norm_store(o1 - lam * o2, sg_ref, o_ref, lambda_init)


def _diff_attn_exact_kernel(q_ref, k_ref, v_ref, lq1_ref, lk1_ref, lq2_ref, lk2_ref, sg_ref, o_ref, *, lambda_init):
    lam = _diff_lambda(lq1_ref, lk1_ref, lq2_ref, lk2_ref, lambda_init)
    k = k_ref[0]

    def softmax_parts(s):
        e = jnp.exp2(s - jnp.max(s, axis=-1, keepdims=True))
        return e, 1.0 / jnp.sum(e, axis=-1, keepdims=True)

    q1, q2 = _map_queries(q_ref[0])
    e1, r1 = softmax_parts(_dot_nt(q1, k))
    e2, r2 = softmax_parts(_dot_nt(q2, k))
    a = e1 * r1 - e2 * (r2 * lam)
    _sublayer_norm_store(_dot(a.astype(BF16), v_ref[0]), sg_ref, o_ref, lambda_init)


def diff_attn(q, k, v, lq1, lk1, lq2, lk2, sub_g, lambda_init, tq, streaming):
    b, seq, d = q.shape
    hw = 2 * DIFF_DH
    small = lambda n: pl.BlockSpec((1, n), lambda i, h, j: (0, 0))
    if streaming:
        kern = functools.partial(_diff_attn_stream_kernel, lambda_init=lambda_init, tk=min(256, seq), lookahead=2)
    else:
        kern = functools.partial(_diff_attn_exact_kernel, lambda_init=lambda_init)
    return pl.pallas_call(
        kern,
        grid=(b, DIFF_HEADS, seq // tq),
        in_specs=[pl.BlockSpec((1, tq, hw), lambda i, h, j: (i, j, h)),
                  pl.BlockSpec((1, seq, hw), lambda i, h, j: (i, 0, h)),
                  pl.BlockSpec((1, seq, hw), lambda i, h, j: (i, 0, h)),
                  small(DIFF_DH), small(DIFF_DH), small(DIFF_DH), small(DIFF_DH), small(hw)],
        out_specs=pl.BlockSpec((1, tq, hw), lambda i, h, j: (i, j, h)),
        out_shape=jax.ShapeDtypeStruct((b, seq, d), BF16),
        compiler_params=_params("parallel", "parallel", "parallel"),
        name="diff_attn_stream" if streaming else "diff_attn_exact",
    )(q, k, v, lq1, lk1, lq2, lk2, sub_g)


def _mem_kv_kernel(m_ref, g_ref, w_ref, kg_ref, k_ref, v_ref):
    d = m_ref.shape[-1]
    m = m_ref[0]
    mn = (m * lax.rsqrt(jnp.mean(m * m, axis=-1, keepdims=True) + EPS) * g_ref[...]).astype(BF16)
    kv = _dot(mn, w_ref[...])
    mdh = d // MEM_HEADS
    for h in range(MEM_HEADS):
        seg = kv[:, h * mdh:(h + 1) * mdh]
        segn = seg * lax.rsqrt(jnp.mean(seg * seg, axis=-1, keepdims=True) + EPS) * kg_ref[:, h * mdh:(h + 1) * mdh]
        k_ref[0, :, h * mdh:(h + 1) * mdh] = segn.astype(BF16)
    v_ref[0] = kv[:, d:].astype(BF16)


def mem_kv(mem, g, w_kv, kg):
    b, m, d = mem.shape
    blk = pl.BlockSpec((1, m, d), lambda i: (i, 0, 0))
    return pl.pallas_call(
        _mem_kv_kernel,
        grid=(b,),
        in_specs=[blk, _resident((1, d)), _resident((d, 2 * d)), _resident((1, d))],
        out_specs=[blk, blk],
        out_shape=[jax.ShapeDtypeStruct((b, m, d), BF16)] * 2,
        compiler_params=_params("parallel"),
        name="mem_kv",
    )(mem, g, w_kv, kg)


def _merge_kernel(x_ref, y_ref, do_ref, qm_ref, gt_ref, mk_ref, mv_ref, wglu_ref, wb_ref, wo_ref, o_ref):
    d = x_ref.shape[-1]
    z = jax.nn.gelu(y_ref[0].astype(F32))
    s5o = (z * jax.nn.sigmoid(_dot(z.astype(BF16), wglu_ref[...]))).astype(BF16)

    qm = qm_ref[0]
    mk = mk_ref[0]
    mv = mv_ref[0]
    mdh = d // MEM_HEADS
    mem_parts = []
    for h in range(MEM_HEADS):
        sl = slice(h * mdh, (h + 1) * mdh)
        s = _dot_nt(qm[:, sl], mk[:, sl])
        e = jnp.exp(s - jnp.max(s, axis=-1, keepdims=True))
        pr = e * (1.0 / jnp.sum(e, axis=-1, keepdims=True))
        mem_parts.append(_dot(pr.astype(BF16), mv[:, sl]).astype(BF16))
    mo = jnp.concatenate(mem_parts, axis=-1)

    merged = None
    for n, br in enumerate((s5o, do_ref[0], mo)):
        term = gt_ref[0, :, n * d:(n + 1) * d].astype(F32) * _dot(br, wb_ref[n])
        merged = term if merged is None else merged + term
    o_ref[0] = x_ref[0] + _dot(merged.astype(BF16), wo_ref[...])


def merge(x, y, do, qm, gates, mk, mv, w_glu, w_branch, w_out, tt):
    b, seq, d = x.shape
    m = mk.shape[1]
    tok = lambda width: pl.BlockSpec((1, tt, width), lambda i, j: (i, j, 0))
    memb = pl.BlockSpec((1, m, d), lambda i, j: (i, 0, 0))
    return pl.pallas_call(
        _merge_kernel,
        grid=(b, seq // tt),
        in_specs=[tok(d), tok(d), tok(d), tok(d), tok(3 * d), memb, memb,
                  _resident((d, d)), _resident((3, d, d)), _resident((d, d))],
        out_specs=tok(d),
        out_shape=jax.ShapeDtypeStruct((b, seq, d), F32),
        compiler_params=_params("parallel", "parallel"),
        name="merge",
    )(x, y, do, qm, gates, mk, mv, w_glu, w_branch, w_out)


def _ffn_kernel(x_ref, g_ref, wgu_ref, wd_ref, o_ref):
    x = x_ref[0]
    dff = wd_ref.shape[0]
    hn = (x * lax.rsqrt(jnp.mean(x * x, axis=-1, keepdims=True) + EPS) * g_ref[...]).astype(BF16)
    gate = _dot(hn, wgu_ref[:, :dff])
    up = _dot(hn, wgu_ref[:, dff:])
    act = (jax.nn.silu(gate) * up).astype(BF16)
    o_ref[0] = x + _dot(act, wd_ref[...])


def ffn(x, g, w_gate_up, w_down, tt):
    b, seq, d = x.shape
    dff = w_down.shape[0]
    tok = pl.BlockSpec((1, tt, d), lambda i, j: (i, j, 0))
    return pl.pallas_call(
        _ffn_kernel,
        grid=(b, seq // tt),
        in_specs=[tok, _resident((1, d)), _resident((d, 2 * dff)), _resident((dff, d))],
        out_specs=tok,
        out_shape=jax.ShapeDtypeStruct((b, seq, d), F32),
        compiler_params=_params("parallel", "parallel"),
        name="ffn",
    )(x, g, w_gate_up, w_down)


def _s5_mixer(u, s5_ops, perm):
    b, seq, d = u.shape
    nc = seq // CHUNK
    cb = min(32, nc)
    w1, w2, a = s5_ops
    u_tm = jnp.swapaxes(u, 0, 1).reshape(nc, CHUNK, b, d)
    chunk_rows = jax.ShapeDtypeStruct((d // S5_GROUP, nc * b, CHUNK * S5_GROUP), BF16)
    u2 = _chunk_relayout(_to_chunks_kernel, u_tm, perm, chunk_rows, cb, True)
    y2 = s5_chunk(u2, w1, w2, a, b, 2)
    y_tm = _chunk_relayout(_from_chunks_kernel, y2, perm, jax.ShapeDtypeStruct(u_tm.shape, BF16), cb, False)
    return jnp.swapaxes(y_tm.reshape(seq, b, d), 0, 1)


def _layer(x, mem, p, li, s5_ops, tables, perm):
    b, seq, d = x.shape
    tt = min(256, seq)
    row = lambda a: a.reshape(1, -1)
    tile_to_d = lambda a: jnp.tile(a, d // a.shape[0]).reshape(1, d)
    heads16 = lax.broadcasted_iota(jnp.int32, (d, d), 0) // DIFF_DH == lax.broadcasted_iota(jnp.int32, (d, d), 1) // DIFF_DH
    bd = jnp.where(heads16, 1.0 / DIFF_DH, 0.0).astype(BF16)
    cos, sin = tables

    u, q, k, v, qm, gates = proj(x, row(p["norm_mix_g"][li]), p["w_in"][li].astype(BF16), bd, row(p["b_gate"][li]),
                                 tile_to_d(p["diff_q_g"][li]), tile_to_d(p["diff_k_g"][li]), tile_to_d(p["mem_q_g"][li]),
                                 cos[:seq], sin[:seq], tt)

    y = _s5_mixer(u, s5_ops, perm)

    lambda_init = 0.8 - 0.6 * math.exp(-0.3 * li)
    attn_args = (q, k, v, row(p["diff_lq1"][li]), row(p["diff_lk1"][li]), row(p["diff_lq2"][li]),
                 row(p["diff_lk2"][li]), row(p["diff_sub_g"][li]))
    score_bound = (1.05 * DIFF_DH ** 0.5 * LOG2E) * jnp.max(jnp.abs(p["diff_q_g"][li])) * jnp.max(jnp.abs(p["diff_k_g"][li]))
    do = lax.cond(score_bound <= SCORE_BOUND,
                  lambda *a: diff_attn(*a, lambda_init, tt, True),
                  lambda *a: diff_attn(*a, lambda_init, tt, False), *attn_args)

    mk, mv = mem_kv(mem, row(p["mem_norm_g"][li]), p["w_mem_kv"][li].astype(BF16), tile_to_d(p["mem_k_g"][li]))
    x1 = merge(x, y, do, qm, gates, mk, mv, p["s5_w_glu"][li].astype(BF16), p["w_branch"][li].astype(BF16),
               p["w_out"][li].astype(BF16), tt)
    return ffn(x1, row(p["ffn_norm_g"][li]), p["w_gate_up"][li].astype(BF16), p["w_down"][li].astype(BF16), tt)


def kernel(x_prompt, x_sample, mem_prompt, mem_sample, norm_mix_g, w_in, b_gate, s5_lam_re, s5_lam_im, s5_log_dt, s5_b_re, s5_b_im, s5_c_re, s5_c_im, s5_d, s5_w_glu, diff_q_g, diff_k_g, diff_lq1, diff_lk1, diff_lq2, diff_lk2, diff_sub_g, mem_norm_g, w_mem_kv, mem_q_g, mem_k_g, w_branch, w_out, ffn_norm_g, w_gate_up, w_down):
    p = dict(norm_mix_g=norm_mix_g, w_in=w_in, b_gate=b_gate, s5_w_glu=s5_w_glu, diff_q_g=diff_q_g, diff_k_g=diff_k_g,
             diff_lq1=diff_lq1, diff_lk1=diff_lk1, diff_lq2=diff_lq2, diff_lk2=diff_lk2, diff_sub_g=diff_sub_g,
             mem_norm_g=mem_norm_g, w_mem_kv=w_mem_kv, mem_q_g=mem_q_g, mem_k_g=mem_k_g, w_branch=w_branch,
             w_out=w_out, ffn_norm_g=ffn_norm_g, w_gate_up=w_gate_up, w_down=w_down)
    tables = rope_tables(max(x_prompt.shape[1], x_sample.shape[1]))
    perm = jnp.asarray(_chunk_permutations(), BF16)
    y_prompt, y_sample = x_prompt, x_sample
    for li in range(norm_mix_g.shape[0]):
        s5_ops = s5_prep(s5_lam_re[li], s5_lam_im[li], s5_log_dt[li], s5_b_re[li], s5_b_im[li],
                         s5_c_re[li], s5_c_im[li], s5_d[li])
        y_prompt = _layer(y_prompt, mem_prompt, p, li, s5_ops, tables, perm)
        y_sample = _layer(y_sample, mem_sample, p, li, s5_ops, tables, perm)
    return (y_prompt, y_sample)
```

```python
import functools
import math

import numpy as np
import jax
import jax.numpy as jnp
from jax import lax
from jax.experimental import pallas as pl
from jax.experimental.pallas import tpu as pltpu

F32 = jnp.float32
BF16 = jnp.bfloat16
EPS = 1e-6
ROPE_THETA = 10000.0
LOG2E = math.log2(math.e)

S5_GROUP = 16
S5_STATE = 64
CHUNK = 16
DIFF_HEADS = 8
DIFF_DH = 64
MEM_HEADS = 4
LANES = 128
GROUPS_PER_LANE_BLOCK = LANES // S5_GROUP
VMEM_LIMIT = 56 * 1024 * 1024
SCORE_BOUND = 60.0

HIGHEST = lax.Precision.HIGHEST


def _dot(a, b):
    return jnp.dot(a, b, preferred_element_type=F32)


def _dot_nt(a, b):
    return lax.dot_general(a, b, (((1,), (1,)), ((), ())), preferred_element_type=F32)


def _dot_hi(a, b):
    return jnp.dot(a, b, precision=HIGHEST, preferred_element_type=F32)


def _params(*sem):
    return pltpu.CompilerParams(dimension_semantics=sem, vmem_limit_bytes=VMEM_LIMIT)


def _resident(shape):
    nd = len(shape)
    return pl.BlockSpec(shape, lambda *_: (0,) * nd, pipeline_mode=pl.Buffered(1))


def _rope_kernel(cos_ref, sin_ref):
    rows = cos_ref.shape[0]
    base = pl.program_id(0) * rows
    pos = (lax.broadcasted_iota(jnp.int32, (rows, LANES), 0) + base).astype(F32)
    lane = lax.broadcasted_iota(jnp.int32, (rows, LANES), 1)
    half = DIFF_DH // 2
    freq = (lane % half).astype(F32)
    inv = jnp.exp(freq * (-math.log(ROPE_THETA) / half))
    ang = pos * inv
    cos_ref[...] = jnp.cos(ang)
    s = jnp.sin(ang)
    sin_ref[...] = jnp.where((lane % DIFF_DH) < half, -s, s)


def rope_tables(seq):
    rows = min(seq, 512)
    return pl.pallas_call(
        _rope_kernel,
        grid=(seq // rows,),
        out_specs=[pl.BlockSpec((rows, LANES), lambda i: (i, 0))] * 2,
        out_shape=[jax.ShapeDtypeStruct((seq, LANES), F32)] * 2,
        name="rope_tables",
    )()


def _proj_kernel(x_ref, g_ref, w_ref, hsel_ref, hexp_ref, bg_ref, qg_ref, kg_ref, mqg_ref, cos_ref, sin_ref,
                 u_ref, q_ref, k_ref, v_ref, qm_ref, gt_ref):
    d = x_ref.shape[-1]
    x = x_ref[0]
    hn = (x * lax.rsqrt(jnp.mean(x * x, axis=-1, keepdims=True) + EPS) * g_ref[...]).astype(BF16)

    def mm(j):
        return _dot(hn, w_ref[:, j * d:(j + 1) * d])

    u_ref[0] = mm(0).astype(BF16)

    cos = cos_ref[...]
    sin = sin_ref[...]
    lane = lax.broadcasted_iota(jnp.int32, cos.shape, 1)
    first_half = (lane % DIFF_DH) < (DIFF_DH // 2)

    def qk_norm_rope(t, gain, scale, out_ref):
        ms = _dot(_dot((t * t).astype(BF16), hsel_ref[...]).astype(BF16), hexp_ref[...])
        tn = t * lax.rsqrt(ms + EPS) * gain
        for j in range(d // LANES):
            blk = tn[:, j * LANES:(j + 1) * LANES]
            partner = jnp.where(first_half, pltpu.roll(blk, LANES - DIFF_DH // 2, 1), pltpu.roll(blk, DIFF_DH // 2, 1))
            out_ref[0, :, j * LANES:(j + 1) * LANES] = ((blk * cos + partner * sin) * scale).astype(BF16)

    qk_norm_rope(mm(1), qg_ref[...], DIFF_DH ** -0.5 * LOG2E, q_ref)
    qk_norm_rope(mm(2), kg_ref[...], 1.0, k_ref)
    v_ref[0] = mm(3).astype(BF16)

    qm = mm(4)
    mdh = d // MEM_HEADS
    for h in range(MEM_HEADS):
        seg = qm[:, h * mdh:(h + 1) * mdh]
        segn = seg * lax.rsqrt(jnp.mean(seg * seg, axis=-1, keepdims=True) + EPS) * mqg_ref[:, h * mdh:(h + 1) * mdh]
        qm_ref[0, :, h * mdh:(h + 1) * mdh] = (segn * (mdh ** -0.5)).astype(BF16)

    for j in range(3):
        gl = mm(5 + j) + bg_ref[:, j * d:(j + 1) * d]
        gt_ref[0, :, j * d:(j + 1) * d] = jax.nn.sigmoid(gl).astype(BF16)


def proj(x, g, w_in, hsel, hexp, b_gate, qg, kg, mqg, cos, sin, tt):
    b, seq, d = x.shape
    ncol = w_in.shape[1]
    tok = lambda width: pl.BlockSpec((1, tt, width), lambda i, j: (i, j, 0))
    outs = [d, d, d, d, d, 3 * d]
    return pl.pallas_call(
        _proj_kernel,
        grid=(b, seq // tt),
        in_specs=[tok(d), _resident((1, d)), _resident((d, ncol)), _resident((d, LANES)), _resident((LANES, d)),
                  _resident((1, 3 * d)),
                  _resident((1, d)), _resident((1, d)), _resident((1, d)),
                  pl.BlockSpec((tt, LANES), lambda i, j: (j, 0)), pl.BlockSpec((tt, LANES), lambda i, j: (j, 0))],
        out_specs=[tok(w) for w in outs],
        out_shape=[jax.ShapeDtypeStruct((b, seq, w), BF16) for w in outs],
        compiler_params=_params("parallel", "parallel"),
        name="proj",
    )(x, g, w_in, hsel, hexp, b_gate, qg, kg, mqg, cos, sin)


def _s5_prep_kernel(lr_ref, li_ref, ldt_ref, btr_ref, bti_ref, lcr_ref, lci_ref, ldtc_ref, ctr_ref, cti_ref, d_ref,
                    w1_ref, w2_ref, a_ref):
    p, h, c = S5_STATE, S5_GROUP, CHUNK
    width = c * h
    lane128 = lax.broadcasted_iota(jnp.int32, (1, LANES), 1)
    fwd128 = lane128 < p

    dt = jnp.exp(ldt_ref[0])
    zr, zi = lr_ref[0] * dt, li_ref[0] * dt
    kk = jnp.minimum(lax.broadcasted_iota(jnp.int32, (24, 1), 0), c).astype(F32)
    mag = jnp.exp(kk * zr)
    pwr, pwi = mag * jnp.cos(kk * zi), mag * jnp.sin(kk * zi)

    lr, li = lr_ref[0], li_ref[0]
    nr, ni = pwr[1:2] - 1.0, pwi[1:2]
    den = lr * lr + li * li
    cr, ci = (nr * lr + ni * li) / den, (ni * lr - nr * li) / den
    btr, bti = btr_ref[0], bti_ref[0]
    bbr, bbi = cr * btr - ci * bti, cr * bti + ci * btr

    for s in range(c):
        sr = jnp.where(fwd128, pwr[c - 1 - s:c - s], pwr[s:s + 1])
        si = jnp.where(fwd128, pwi[c - 1 - s:c - s], pwi[s:s + 1])
        w1_ref[0, s * h:(s + 1) * h, width:width + LANES] = (bbr * sr - bbi * si).astype(BF16)
        w1_ref[0, s * h:(s + 1) * h, width + LANES:width + 2 * LANES] = (bbr * si + bbi * sr).astype(BF16)

    krow = jnp.minimum(lax.broadcasted_iota(jnp.int32, (1, LANES), 1), c).astype(F32)
    sel_k = lax.broadcasted_iota(jnp.int32, (LANES, width), 0)
    sel_t = lax.broadcasted_iota(jnp.int32, (LANES, width), 1) // h

    def expand(pw, power_of_t):
        return _dot_hi(pw, (sel_k == power_of_t(sel_t)).astype(F32))

    zr_stack, zi_stack = [], []
    for dr in range(2):
        dtc = jnp.exp(ldtc_ref[dr, 0])
        zrc, zic = lcr_ref[dr, 0] * dtc, lci_ref[dr, 0] * dtc
        magc = jnp.exp(zrc * krow)
        pcr, pci = magc * jnp.cos(zic * krow), magc * jnp.sin(zic * krow)
        ctr, cti = ctr_ref[dr, 0], cti_ref[dr, 0]
        lag = (lambda t: t) if dr == 0 else (lambda t: c - 1 - t)
        er, ei = expand(pcr, lag), expand(pci, lag)
        zr_stack.append(er * ctr - ei * cti)
        zi_stack.append(er * cti + ei * ctr)
        rd = (lambda t: t + 1) if dr == 0 else (lambda t: c - t)
        er, ei = expand(pcr, rd), expand(pci, rd)
        re_rows = (er * ctr - ei * cti).astype(BF16)
        im_rows = (-(er * cti + ei * ctr)).astype(BF16)
        w2_ref[0, dr * p:(dr + 1) * p] = re_rows
        w2_ref[0, LANES + dr * p:LANES + (dr + 1) * p] = im_rows

    zr_all = jnp.concatenate(zr_stack, axis=0)
    zi_all = jnp.concatenate(zi_stack, axis=0)
    lane16 = lax.broadcasted_iota(jnp.int32, (h, LANES), 1)
    g_f = _dot_hi(jnp.where(lane16 < p, bbr, 0.0), zr_all) - _dot_hi(jnp.where(lane16 < p, bbi, 0.0), zi_all)
    g_b = _dot_hi(jnp.where(lane16 < p, 0.0, bbr), zr_all) - _dot_hi(jnp.where(lane16 < p, 0.0, bbi), zi_all)

    lane = lax.broadcasted_iota(jnp.int32, (h, width), 1)
    row = lax.broadcasted_iota(jnp.int32, (h, width), 0)
    dcol = d_ref[0]
    for s in range(c):
        m_f = jnp.where(lane >= h * s, g_f if s == 0 else pltpu.roll(g_f, h * s, 1), 0.0)
        m_b = jnp.where(lane < h * (s + 1), g_b if s == c - 1 else pltpu.roll(g_b, h * (s + 1), 1), 0.0)
        skip = jnp.where(lane == h * s + row, dcol, 0.0)
        w1_ref[0, s * h:(s + 1) * h, 0:width] = (m_f + m_b + skip).astype(BF16)

    r8 = lax.broadcasted_iota(jnp.int32, (8, LANES), 0)
    a_ref[0] = jnp.where(r8 == 0, pwr[c:c + 1], jnp.where(r8 == 1, pwi[c:c + 1], 0.0))


def s5_prep(lam_re, lam_im, log_dt, b_re, b_im, c_re, c_im, d):
    _, g, p = lam_re.shape
    h = S5_GROUP
    width = CHUNK * h
    pack = lambda a: jnp.concatenate([a[0], a[1]], axis=-1)
    lr = pack(lam_re)[:, None, :]
    li = pack(lam_im)[:, None, :]
    ldt = pack(jnp.broadcast_to(log_dt[..., None], (2, g, p)))[:, None, :]
    btr = pack(jnp.swapaxes(b_re, -1, -2))
    bti = pack(jnp.swapaxes(b_im, -1, -2))
    ctr = jnp.tile(jnp.swapaxes(c_re, -1, -2), (1, 1, 1, CHUNK))
    cti = jnp.tile(jnp.swapaxes(c_im, -1, -2), (1, 1, 1, CHUNK))
    row = lambda last: pl.BlockSpec((1, 1, last), lambda i: (i, 0, 0))
    col = lambda rows, last: pl.BlockSpec((2, 1, rows, last), lambda i: (0, i, 0, 0))
    return pl.pallas_call(
        _s5_prep_kernel,
        grid=(g,),
        in_specs=[row(LANES), row(LANES), row(LANES),
                  pl.BlockSpec((1, h, LANES), lambda i: (i, 0, 0)), pl.BlockSpec((1, h, LANES), lambda i: (i, 0, 0)),
                  col(p, 1), col(p, 1), col(1, 1), col(p, width), col(p, width),
                  pl.BlockSpec((1, h, 1), lambda i: (i, 0, 0))],
        out_specs=[pl.BlockSpec((1, width, 2 * width), lambda i: (i, 0, 0)),
                   pl.BlockSpec((1, width, width), lambda i: (i, 0, 0)),
                   pl.BlockSpec((1, 8, LANES), lambda i: (i, 0, 0))],
        out_shape=[jax.ShapeDtypeStruct((g, width, 2 * width), BF16),
                   jax.ShapeDtypeStruct((g, width, width), BF16),
                   jax.ShapeDtypeStruct((g, 8, LANES), F32)],
        compiler_params=_params("parallel"),
        name="s5_prep",
    )(lr, li, ldt, btr, bti, lam_re[..., None], lam_im[..., None], log_dt[..., None, None], ctr, cti,
      d.reshape(g, h, 1))


def _chunk_permutation():
    n, half = GROUPS_PER_LANE_BLOCK, CHUNK // 2
    pm = np.zeros((half * LANES, n * half * S5_GROUP), np.float32)
    for t in range(half):
        for g in range(n):
            for h in range(S5_GROUP):
                pm[t * LANES + g * S5_GROUP + h, g * LANES + t * S5_GROUP + h] = 1.0
    return pm


def _to_chunks_kernel(x_ref, p_ref, o_ref):
    cb, _, bsz, _ = x_ref.shape
    rows, half = cb * bsz, CHUNK // 2
    for hf in range(2):
        x8 = jnp.concatenate([x_ref[:, hf * half + t].reshape(rows, LANES) for t in range(half)], axis=1)
        out = _dot(x8, p_ref[...])
        for g in range(GROUPS_PER_LANE_BLOCK):
            o_ref[g, :, hf * LANES:(hf + 1) * LANES] = out[:, g * LANES:(g + 1) * LANES].astype(BF16)


def _from_chunks_kernel(y_ref, p_ref, o_ref):
    cb, _, bsz, _ = o_ref.shape
    half = CHUNK // 2
    for hf in range(2):
        y8 = jnp.concatenate([y_ref[g, :, hf * LANES:(hf + 1) * LANES] for g in range(GROUPS_PER_LANE_BLOCK)], axis=1)
        out = _dot_nt(y8, p_ref[...])
        for t in range(half):
            o_ref[:, hf * half + t] = out[:, t * LANES:(t + 1) * LANES].astype(BF16).reshape(cb, bsz, LANES)


def _chunk_relayout(kern, x, perm, out_struct, cb, to_chunks):
    n = GROUPS_PER_LANE_BLOCK
    if to_chunks:
        nc, _, bsz, d = x.shape
    else:
        nc, _, bsz, d = out_struct.shape
    tm_spec = pl.BlockSpec((cb, CHUNK, bsz, LANES), lambda i, j: (i, 0, 0, j))
    ch_spec = pl.BlockSpec((n, cb * bsz, CHUNK * S5_GROUP), lambda i, j: (j, i, 0))
    return pl.pallas_call(
        kern,
        grid=(nc // cb, d // LANES),
        in_specs=[tm_spec if to_chunks else ch_spec, _resident(perm.shape)],
        out_specs=ch_spec if to_chunks else tm_spec,
        out_shape=out_struct,
        compiler_params=_params("parallel", "parallel"),
        name="to_chunks" if to_chunks else "from_chunks",
    )(x, perm)


def _s5_chunk_kernel(u_ref, w1_ref, w2_ref, a_ref, y_ref, yint_ref, sin_ref, st_ref, *, rows_per_chunk):
    gb, nrows, width = u_ref.shape
    rb = rows_per_chunk
    n_chunks = nrows // rb
    ns = S5_STATE
    fwd = lax.broadcasted_iota(jnp.int32, (rb, LANES), 1) < ns

    for g in range(gb):
        r = _dot(u_ref[g], w1_ref[g])
        yint_ref[...] = r[:, :width]
        sin_ref[...] = r[:, width:]
        a_re = a_ref[g, 0:1, :]
        a_im = a_ref[g, 1:2, :]

        def step(i, carry):
            s_re, s_im = carry
            rf = pl.ds(pl.multiple_of(i * rb, rb), rb)
            rr = pl.ds(pl.multiple_of((n_chunks - 1 - i) * rb, rb), rb)
            st_ref[rf, 0:ns] = s_re[:, 0:ns]
            st_ref[rf, LANES:LANES + ns] = s_im[:, 0:ns]
            st_ref[rr, ns:LANES] = s_re[:, ns:LANES]
            st_ref[rr, LANES + ns:2 * LANES] = s_im[:, ns:LANES]
            in_re = jnp.where(fwd, sin_ref[rf, 0:LANES], sin_ref[rr, 0:LANES])
            in_im = jnp.where(fwd, sin_ref[rf, LANES:2 * LANES], sin_ref[rr, LANES:2 * LANES])
            return (a_re * s_re - a_im * s_im + in_re, a_re * s_im + a_im * s_re + in_im)

        zero = jnp.zeros((rb, LANES), F32)
        lax.fori_loop(0, n_chunks, step, (zero, zero), unroll=4)
        y_ref[g] = (yint_ref[...] + _dot(st_ref[...].astype(BF16), w2_ref[g])).astype(BF16)


def s5_chunk(u2, w1, w2, a, rows_per_chunk, gb):
    g, nrows, width = u2.shape
    kern = functools.partial(_s5_chunk_kernel, rows_per_chunk=rows_per_chunk)
    return pl.pallas_call(
        kern,
        grid=(g // gb,),
        in_specs=[pl.BlockSpec((gb, nrows, width), lambda i: (i, 0, 0)),
                  pl.BlockSpec((gb, width, 2 * width), lambda i: (i, 0, 0)),
                  pl.BlockSpec((gb, width, width), lambda i: (i, 0, 0)),
                  pl.BlockSpec((gb, 8, LANES), lambda i: (i, 0, 0))],
        out_specs=pl.BlockSpec((gb, nrows, width), lambda i: (i, 0, 0)),
        out_shape=jax.ShapeDtypeStruct((g, nrows, width), BF16),
        scratch_shapes=[pltpu.VMEM((nrows, width), F32), pltpu.VMEM((nrows, width), F32),
                        pltpu.VMEM((nrows, width), F32)],
        compiler_params=_params("parallel"),
        name="s5_chunk",
    )(u2, w1, w2, a)


def _diff_lambda(lq1_ref, lk1_ref, lq2_ref, lk2_ref, lambda_init):
    return (jnp.exp(jnp.sum(lq1_ref[...] * lk1_ref[...], axis=-1, keepdims=True))
            - jnp.exp(jnp.sum(lq2_ref[...] * lk2_ref[...], axis=-1, keepdims=True)) + lambda_init)


def _map_queries(q):
    lane = lax.broadcasted_iota(jnp.int32, q.shape, 1)
    zero = jnp.zeros_like(q)
    return jnp.where(lane < DIFF_DH, q, zero), jnp.where(lane < DIFF_DH, zero, q)


def _sublayer_norm_store(o, sg_ref, o_ref, lambda_init):
    on = o * lax.rsqrt(jnp.mean(o * o, axis=-1, keepdims=True) + EPS) * sg_ref[...]
    o_ref[0] = (on * (1.0 - lambda_init)).astype(BF16)


def _diff_attn_stream_kernel(q_ref, k_ref, v_ref, lq1_ref, lk1_ref, lq2_ref, lk2_ref, sg_ref, o_ref, e_ref, *,
                             lambda_init, tk, sub):
    tq = q_ref.shape[1]
    seq = k_ref.shape[1]
    lam = _diff_lambda(lq1_ref, lk1_ref, lq2_ref, lk2_ref, lambda_init)
    n_sub = tq // sub

    sums = []
    for h in range(n_sub):
        qmaps = _map_queries(q_ref[0, h * sub:(h + 1) * sub, :])
        part = [jnp.zeros((sub, LANES), F32), jnp.zeros((sub, LANES), F32)]
        for kb in range(seq // tk):
            for mp in (0, 1):
                e = jnp.exp2(_dot_nt(qmaps[mp], k_ref[0, kb * tk:(kb + 1) * tk, :]))
                cols = [e[:, c * LANES:(c + 1) * LANES] for c in range(tk // LANES)]
                while len(cols) > 1:
                    cols = [cols[i] + cols[i + 1] for i in range(0, len(cols), 2)]
                part[mp] = part[mp] + cols[0]
                e_ref[h, mp, :, kb * tk:(kb + 1) * tk] = e.astype(BF16)
        sums.append([jnp.sum(pt, axis=-1, keepdims=True) for pt in part])

    for h in range(n_sub):
        l1, l2 = sums[h]
        mu = jnp.broadcast_to(lam * l1 / l2, (sub, LANES)).astype(BF16)
        acc = None
        for kb in range(seq // tk):
            cols = []
            for c in range(tk // LANES):
                sl = slice(kb * tk + c * LANES, kb * tk + (c + 1) * LANES)
                cols.append(e_ref[h, 0, :, sl] - mu * e_ref[h, 1, :, sl])
            term = _dot(jnp.concatenate(cols, axis=1), v_ref[0, kb * tk:(kb + 1) * tk, :])
            acc = term if acc is None else acc + term
        o = acc * (1.0 / l1)
        on = o * lax.rsqrt(jnp.mean(o * o, axis=-1, keepdims=True) + EPS) * sg_ref[...]
        o_ref[0, h * sub:(h + 1) * sub, :] = (on * (1.0 - lambda_init)).astype(BF16)


def _diff_attn_exact_kernel(q_ref, k_ref, v_ref, lq1_ref, lk1_ref, lq2_ref, lk2_ref, sg_ref, o_ref, *, lambda_init):
    lam = _diff_lambda(lq1_ref, lk1_ref, lq2_ref, lk2_ref, lambda_init)
    k = k_ref[0]

    def softmax_parts(s):
        e = jnp.exp2(s - jnp.max(s, axis=-1, keepdims=True))
        return e, 1.0 / jnp.sum(e, axis=-1, keepdims=True)

    q1, q2 = _map_queries(q_ref[0])
    e1, r1 = softmax_parts(_dot_nt(q1, k))
    e2, r2 = softmax_parts(_dot_nt(q2, k))
    a = e1 * r1 - e2 * (r2 * lam)
    _sublayer_norm_store(_dot(a.astype(BF16), v_ref[0]), sg_ref, o_ref, lambda_init)


def diff_attn(q, k, v, lq1, lk1, lq2, lk2, sub_g, lambda_init, tq, streaming):
    b, seq, d = q.shape
    hw = 2 * DIFF_DH
    small = lambda n: pl.BlockSpec((1, n), lambda i, h, j: (0, 0))
    scratch = []
    if streaming:
        sub = tq
        tq = min(2 * sub, seq)
        kern = functools.partial(_diff_attn_stream_kernel, lambda_init=lambda_init, tk=min(512, seq), sub=sub)
        scratch = [pltpu.VMEM((tq // sub, 2, sub, seq), BF16)]
    else:
        kern = functools.partial(_diff_attn_exact_kernel, lambda_init=lambda_init)
    return pl.pallas_call(
        kern,
        grid=(b, DIFF_HEADS, seq // tq),
        in_specs=[pl.BlockSpec((1, tq, hw), lambda i, h, j: (i, j, h)),
                  pl.BlockSpec((1, seq, hw), lambda i, h, j: (i, 0, h)),
                  pl.BlockSpec((1, seq, hw), lambda i, h, j: (i, 0, h)),
                  small(DIFF_DH), small(DIFF_DH), small(DIFF_DH), small(DIFF_DH), small(hw)],
        out_specs=pl.BlockSpec((1, tq, hw), lambda i, h, j: (i, j, h)),
        out_shape=jax.ShapeDtypeStruct((b, seq, d), BF16),
        scratch_shapes=scratch,
        compiler_params=_params("parallel", "parallel", "parallel"),
        name="diff_attn_stream" if streaming else "diff_attn_exact",
    )(q, k, v, lq1, lk1, lq2, lk2, sub_g)


def _mem_kv_kernel(m_ref, g_ref, w_ref, kg_ref, k_ref, v_ref):
    d = m_ref.shape[-1]
    m = m_ref[0]
    mn = (m * lax.rsqrt(jnp.mean(m * m, axis=-1, keepdims=True) + EPS) * g_ref[...]).astype(BF16)
    kv = _dot(mn, w_ref[...])
    mdh = d // MEM_HEADS
    for h in range(MEM_HEADS):
        seg = kv[:, h * mdh:(h + 1) * mdh]
        segn = seg * lax.rsqrt(jnp.mean(seg * seg, axis=-1, keepdims=True) + EPS) * kg_ref[:, h * mdh:(h + 1) * mdh]
        k_ref[0, :, h * mdh:(h + 1) * mdh] = segn.astype(BF16)
    v_ref[0] = kv[:, d:].astype(BF16)


def mem_kv(mem, g, w_kv, kg):
    b, m, d = mem.shape
    blk = pl.BlockSpec((1, m, d), lambda i: (i, 0, 0))
    return pl.pallas_call(
        _mem_kv_kernel,
        grid=(b,),
        in_specs=[blk, _resident((1, d)), _resident((d, 2 * d)), _resident((1, d))],
        out_specs=[blk, blk],
        out_shape=[jax.ShapeDtypeStruct((b, m, d), BF16)] * 2,
        compiler_params=_params("parallel"),
        name="mem_kv",
    )(mem, g, w_kv, kg)


def _merge_kernel(x_ref, y_ref, do_ref, qm_ref, gt_ref, mk_ref, mv_ref, wglu_ref, wb_ref, wo_ref, o_ref):
    d = x_ref.shape[-1]
    z = jax.nn.gelu(y_ref[0].astype(F32))
    s5o = (z * jax.nn.sigmoid(_dot(z.astype(BF16), wglu_ref[...]))).astype(BF16)

    qm = qm_ref[0]
    mk = mk_ref[0]
    mv = mv_ref[0]
    mdh = d // MEM_HEADS
    mem_parts = []
    for h in range(MEM_HEADS):
        sl = slice(h * mdh, (h + 1) * mdh)
        s = _dot_nt(qm[:, sl], mk[:, sl])
        e = jnp.exp(s - jnp.max(s, axis=-1, keepdims=True))
        pr = e * (1.0 / jnp.sum(e, axis=-1, keepdims=True))
        mem_parts.append(_dot(pr.astype(BF16), mv[:, sl]).astype(BF16))
    mo = jnp.concatenate(mem_parts, axis=-1)

    merged = None
    for n, br in enumerate((s5o, do_ref[0], mo)):
        term = gt_ref[0, :, n * d:(n + 1) * d].astype(F32) * _dot(br, wb_ref[n])
        merged = term if merged is None else merged + term
    o_ref[0] = x_ref[0] + _dot(merged.astype(BF16), wo_ref[...])


def merge(x, y, do, qm, gates, mk, mv, w_glu, w_branch, w_out, tt):
    b, seq, d = x.shape
    m = mk.shape[1]
    tok = lambda width: pl.BlockSpec((1, tt, width), lambda i, j: (i, j, 0))
    memb = pl.BlockSpec((1, m, d), lambda i, j: (i, 0, 0))
    return pl.pallas_call(
        _merge_kernel,
        grid=(b, seq // tt),
        in_specs=[tok(d), tok(d), tok(d), tok(d), tok(3 * d), memb, memb,
                  _resident((d, d)), _resident((3, d, d)), _resident((d, d))],
        out_specs=tok(d),
        out_shape=jax.ShapeDtypeStruct((b, seq, d), F32),
        compiler_params=_params("parallel", "parallel"),
        name="merge",
    )(x, y, do, qm, gates, mk, mv, w_glu, w_branch, w_out)


def _ffn_kernel(x_ref, g_ref, wgu_ref, wd_ref, o_ref):
    x = x_ref[0]
    dff = wd_ref.shape[0]
    hn = (x * lax.rsqrt(jnp.mean(x * x, axis=-1, keepdims=True) + EPS) * g_ref[...]).astype(BF16)
    gate = _dot(hn, wgu_ref[:, :dff])
    up = _dot(hn, wgu_ref[:, dff:])
    act = (jax.nn.silu(gate) * up).astype(BF16)
    o_ref[0] = x + _dot(act, wd_ref[...])


def ffn(x, g, w_gate_up, w_down, tt):
    b, seq, d = x.shape
    dff = w_down.shape[0]
    tok = pl.BlockSpec((1, tt, d), lambda i, j: (i, j, 0))
    return pl.pallas_call(
        _ffn_kernel,
        grid=(b, seq // tt),
        in_specs=[tok, _resident((1, d)), _resident((d, 2 * dff)), _resident((dff, d))],
        out_specs=tok,
        out_shape=jax.ShapeDtypeStruct((b, seq, d), F32),
        compiler_params=_params("parallel", "parallel"),
        name="ffn",
    )(x, g, w_gate_up, w_down)


def _s5_mixer(u, s5_ops, perm):
    b, seq, d = u.shape
    nc = seq // CHUNK
    cb = min(32, nc)
    w1, w2, a = s5_ops
    u_tm = jnp.swapaxes(u, 0, 1).reshape(nc, CHUNK, b, d)
    chunk_rows = jax.ShapeDtypeStruct((d // S5_GROUP, nc * b, CHUNK * S5_GROUP), BF16)
    u2 = _chunk_relayout(_to_chunks_kernel, u_tm, perm, chunk_rows, cb, True)
    y2 = s5_chunk(u2, w1, w2, a, b, 2)
    y_tm = _chunk_relayout(_from_chunks_kernel, y2, perm, jax.ShapeDtypeStruct(u_tm.shape, BF16), cb, False)
    return jnp.swapaxes(y_tm.reshape(seq, b, d), 0, 1)


def _layer(x, mem, p, li, s5_ops, tables, perm):
    b, seq, d = x.shape
    tt = min(256, seq)
    row = lambda a: a.reshape(1, -1)
    tile_to_d = lambda a: jnp.tile(a, d // a.shape[0]).reshape(1, d)
    head_of_lane = (lax.broadcasted_iota(jnp.int32, (d, LANES), 0) // DIFF_DH
                    == lax.broadcasted_iota(jnp.int32, (d, LANES), 1))
    hsel = jnp.where(head_of_lane, 1.0 / DIFF_DH, 0.0).astype(BF16)
    hexp = head_of_lane.T.astype(BF16)
    cos, sin = tables

    u, q, k, v, qm, gates = proj(x, row(p["norm_mix_g"][li]), p["w_in"][li].astype(BF16), hsel, hexp, row(p["b_gate"][li]),
                                 tile_to_d(p["diff_q_g"][li]), tile_to_d(p["diff_k_g"][li]), tile_to_d(p["mem_q_g"][li]),
                                 cos[:seq], sin[:seq], tt)

    y = _s5_mixer(u, s5_ops, perm)

    lambda_init = 0.8 - 0.6 * math.exp(-0.3 * li)
    attn_args = (q, k, v, row(p["diff_lq1"][li]), row(p["diff_lk1"][li]), row(p["diff_lq2"][li]),
                 row(p["diff_lk2"][li]), row(p["diff_sub_g"][li]))
    score_bound = (1.05 * DIFF_DH ** 0.5 * LOG2E) * jnp.max(jnp.abs(p["diff_q_g"][li])) * jnp.max(jnp.abs(p["diff_k_g"][li]))
    do = lax.cond(score_bound <= SCORE_BOUND,
                  lambda *a: diff_attn(*a, lambda_init, tt, True),
                  lambda *a: diff_attn(*a, lambda_init, tt, False), *attn_args)

    mk, mv = mem_kv(mem, row(p["mem_norm_g"][li]), p["w_mem_kv"][li].astype(BF16), tile_to_d(p["mem_k_g"][li]))
    x1 = merge(x, y, do, qm, gates, mk, mv, p["s5_w_glu"][li].astype(BF16), p["w_branch"][li].astype(BF16),
               p["w_out"][li].astype(BF16), tt)
    return ffn(x1, row(p["ffn_norm_g"][li]), p["w_gate_up"][li].astype(BF16), p["w_down"][li].astype(BF16), tt)


def kernel(x_prompt, x_sample, mem_prompt, mem_sample, norm_mix_g, w_in, b_gate, s5_lam_re, s5_lam_im, s5_log_dt, s5_b_re, s5_b_im, s5_c_re, s5_c_im, s5_d, s5_w_glu, diff_q_g, diff_k_g, diff_lq1, diff_lk1, diff_lq2, diff_lk2, diff_sub_g, mem_norm_g, w_mem_kv, mem_q_g, mem_k_g, w_branch, w_out, ffn_norm_g, w_gate_up, w_down):
    p = dict(norm_mix_g=norm_mix_g, w_in=w_in, b_gate=b_gate, s5_w_glu=s5_w_glu, diff_q_g=diff_q_g, diff_k_g=diff_k_g,
             diff_lq1=diff_lq1, diff_lk1=diff_lk1, diff_lq2=diff_lq2, diff_lk2=diff_lk2, diff_sub_g=diff_sub_g,
             mem_norm_g=mem_norm_g, w_mem_kv=w_mem_kv, mem_q_g=mem_q_g, mem_k_g=mem_k_g, w_branch=w_branch,
             w_out=w_out, ffn_norm_g=ffn_norm_g, w_gate_up=w_gate_up, w_down=w_down)
    tables = rope_tables(max(x_prompt.shape[1], x_sample.shape[1]))
    perm = jnp.asarray(_chunk_permutation(), BF16)
    y_prompt, y_sample = x_prompt, x_sample
    for li in range(norm_mix_g.shape[0]):
        s5_ops = s5_prep(s5_lam_re[li], s5_lam_im[li], s5_log_dt[li], s5_b_re[li], s5_b_im[li],
                         s5_c_re[li], s5_c_im[li], s5_d[li])
        y_prompt = _layer(y_prompt, mem_prompt, p, li, s5_ops, tables, perm)
        y_sample = _layer(y_sample, mem_sample, p, li, s5_ops, tables, perm)
    return (y_prompt, y_sample)
```

```python
import functools
import math

import numpy as np
import jax
import jax.numpy as jnp
from jax import lax
from jax.experimental import pallas as pl
from jax.experimental.pallas import tpu as pltpu

F32 = jnp.float32
BF16 = jnp.bfloat16
EPS = 1e-6
ROPE_THETA = 10000.0
LOG2E = math.log2(math.e)

S5_GROUP = 16
S5_STATE = 64
CHUNK = 16
DIFF_HEADS = 8
DIFF_DH = 64
MEM_HEADS = 4
LANES = 128
GROUPS_PER_LANE_BLOCK = LANES // S5_GROUP
VMEM_LIMIT = 56 * 1024 * 1024
DENSE_TOKENS = 512
ATTN_QUERY_ROWS = 256
ATTN_SUBTILES = 4
ATTN_KEY_BLOCK = 512
S5_CHUNKS_PER_STEP = 32
S5_GROUPS_PER_STEP = 2
SCORE_BOUND = 60.0

HIGHEST = lax.Precision.HIGHEST


def _dot(a, b):
    return jnp.dot(a, b, preferred_element_type=F32)


def _dot_nt(a, b):
    return lax.dot_general(a, b, (((1,), (1,)), ((), ())), preferred_element_type=F32)


def _dot_hi(a, b):
    return jnp.dot(a, b, precision=HIGHEST, preferred_element_type=F32)


def _params(*sem):
    return pltpu.CompilerParams(dimension_semantics=sem, vmem_limit_bytes=VMEM_LIMIT)


def _resident(shape):
    nd = len(shape)
    return pl.BlockSpec(shape, lambda *_: (0,) * nd, pipeline_mode=pl.Buffered(1))


def _rope_kernel(cos_ref, sin_ref):
    rows = cos_ref.shape[0]
    base = pl.program_id(0) * rows
    pos = (lax.broadcasted_iota(jnp.int32, (rows, LANES), 0) + base).astype(F32)
    lane = lax.broadcasted_iota(jnp.int32, (rows, LANES), 1)
    half = DIFF_DH // 2
    freq = (lane % half).astype(F32)
    inv = jnp.exp(freq * (-math.log(ROPE_THETA) / half))
    ang = pos * inv
    cos_ref[...] = jnp.cos(ang)
    s = jnp.sin(ang)
    sin_ref[...] = jnp.where((lane % DIFF_DH) < half, -s, s)


def rope_tables(seq):
    rows = min(seq, 512)
    return pl.pallas_call(
        _rope_kernel,
        grid=(seq // rows,),
        out_specs=[pl.BlockSpec((rows, LANES), lambda i: (i, 0))] * 2,
        out_shape=[jax.ShapeDtypeStruct((seq, LANES), F32)] * 2,
        name="rope_tables",
    )()


def _proj_kernel(x_ref, g_ref, w_ref, hsel_ref, hexp_ref, bg_ref, qg_ref, kg_ref, mqg_ref, cos_ref, sin_ref,
                 u_ref, q_ref, k_ref, v_ref, qm_ref, gt_ref):
    d = x_ref.shape[-1]
    x = x_ref[0]
    hn = (x * lax.rsqrt(jnp.mean(x * x, axis=-1, keepdims=True) + EPS) * g_ref[...]).astype(BF16)

    def mm(j):
        return _dot(hn, w_ref[:, j * d:(j + 1) * d])

    u_ref[0] = mm(0).astype(BF16)

    cos = cos_ref[...]
    sin = sin_ref[...]
    lane = lax.broadcasted_iota(jnp.int32, cos.shape, 1)
    first_half = (lane % DIFF_DH) < (DIFF_DH // 2)

    def qk_norm_rope(t, gain, scale, out_ref):
        ms = _dot(_dot((t * t).astype(BF16), hsel_ref[...]).astype(BF16), hexp_ref[...])
        tn = t * lax.rsqrt(ms + EPS) * gain
        for j in range(d // LANES):
            blk = tn[:, j * LANES:(j + 1) * LANES]
            partner = jnp.where(first_half, pltpu.roll(blk, LANES - DIFF_DH // 2, 1), pltpu.roll(blk, DIFF_DH // 2, 1))
            out_ref[0, :, j * LANES:(j + 1) * LANES] = ((blk * cos + partner * sin) * scale).astype(BF16)

    qk_norm_rope(mm(1), qg_ref[...], DIFF_DH ** -0.5 * LOG2E, q_ref)
    qk_norm_rope(mm(2), kg_ref[...], 1.0, k_ref)
    v_ref[0] = mm(3).astype(BF16)

    qm = mm(4)
    mdh = d // MEM_HEADS
    for h in range(MEM_HEADS):
        seg = qm[:, h * mdh:(h + 1) * mdh]
        segn = seg * lax.rsqrt(jnp.mean(seg * seg, axis=-1, keepdims=True) + EPS) * mqg_ref[:, h * mdh:(h + 1) * mdh]
        qm_ref[0, :, h * mdh:(h + 1) * mdh] = (segn * (mdh ** -0.5)).astype(BF16)

    for j in range(3):
        gl = mm(5 + j) + bg_ref[:, j * d:(j + 1) * d]
        gt_ref[0, :, j * d:(j + 1) * d] = jax.nn.sigmoid(gl).astype(BF16)


def proj(x, g, w_in, hsel, hexp, b_gate, qg, kg, mqg, cos, sin, tt):
    b, seq, d = x.shape
    ncol = w_in.shape[1]
    tok = lambda width: pl.BlockSpec((1, tt, width), lambda i, j: (i, j, 0))
    outs = [d, d, d, d, d, 3 * d]
    return pl.pallas_call(
        _proj_kernel,
        grid=(b, seq // tt),
        in_specs=[tok(d), _resident((1, d)), _resident((d, ncol)), _resident((d, LANES)), _resident((LANES, d)),
                  _resident((1, 3 * d)),
                  _resident((1, d)), _resident((1, d)), _resident((1, d)),
                  pl.BlockSpec((tt, LANES), lambda i, j: (j, 0)), pl.BlockSpec((tt, LANES), lambda i, j: (j, 0))],
        out_specs=[tok(w) for w in outs],
        out_shape=[jax.ShapeDtypeStruct((b, seq, w), BF16) for w in outs],
        compiler_params=_params("parallel", "parallel"),
        name="proj",
    )(x, g, w_in, hsel, hexp, b_gate, qg, kg, mqg, cos, sin)


def _s5_prep_kernel(lr_ref, li_ref, ldt_ref, btr_ref, bti_ref, lcr_ref, lci_ref, ldtc_ref, ctr_ref, cti_ref, d_ref,
                    w1_ref, w2_ref, a_ref):
    p, h, c = S5_STATE, S5_GROUP, CHUNK
    width = c * h
    lane128 = lax.broadcasted_iota(jnp.int32, (1, LANES), 1)
    fwd128 = lane128 < p

    dt = jnp.exp(ldt_ref[0])
    zr, zi = lr_ref[0] * dt, li_ref[0] * dt
    kk = jnp.minimum(lax.broadcasted_iota(jnp.int32, (24, 1), 0), c).astype(F32)
    mag = jnp.exp(kk * zr)
    pwr, pwi = mag * jnp.cos(kk * zi), mag * jnp.sin(kk * zi)

    lr, li = lr_ref[0], li_ref[0]
    nr, ni = pwr[1:2] - 1.0, pwi[1:2]
    den = lr * lr + li * li
    cr, ci = (nr * lr + ni * li) / den, (ni * lr - nr * li) / den
    btr, bti = btr_ref[0], bti_ref[0]
    bbr, bbi = cr * btr - ci * bti, cr * bti + ci * btr

    for s in range(c):
        sr = jnp.where(fwd128, pwr[c - 1 - s:c - s], pwr[s:s + 1])
        si = jnp.where(fwd128, pwi[c - 1 - s:c - s], pwi[s:s + 1])
        w1_ref[0, s * h:(s + 1) * h, width:width + LANES] = (bbr * sr - bbi * si).astype(BF16)
        w1_ref[0, s * h:(s + 1) * h, width + LANES:width + 2 * LANES] = (bbr * si + bbi * sr).astype(BF16)

    krow = jnp.minimum(lax.broadcasted_iota(jnp.int32, (1, LANES), 1), c).astype(F32)
    sel_k = lax.broadcasted_iota(jnp.int32, (LANES, width), 0)
    sel_t = lax.broadcasted_iota(jnp.int32, (LANES, width), 1) // h

    def expand(pw, power_of_t):
        return _dot_hi(pw, (sel_k == power_of_t(sel_t)).astype(F32))

    zr_stack, zi_stack = [], []
    for dr in range(2):
        dtc = jnp.exp(ldtc_ref[dr, 0])
        zrc, zic = lcr_ref[dr, 0] * dtc, lci_ref[dr, 0] * dtc
        magc = jnp.exp(zrc * krow)
        pcr, pci = magc * jnp.cos(zic * krow), magc * jnp.sin(zic * krow)
        ctr, cti = ctr_ref[dr, 0], cti_ref[dr, 0]
        lag = (lambda t: t) if dr == 0 else (lambda t: c - 1 - t)
        er, ei = expand(pcr, lag), expand(pci, lag)
        zr_stack.append(er * ctr - ei * cti)
        zi_stack.append(er * cti + ei * ctr)
        rd = (lambda t: t + 1) if dr == 0 else (lambda t: c - t)
        er, ei = expand(pcr, rd), expand(pci, rd)
        re_rows = (er * ctr - ei * cti).astype(BF16)
        im_rows = (-(er * cti + ei * ctr)).astype(BF16)
        w2_ref[0, dr * p:(dr + 1) * p] = re_rows
        w2_ref[0, LANES + dr * p:LANES + (dr + 1) * p] = im_rows

    zr_all = jnp.concatenate(zr_stack, axis=0)
    zi_all = jnp.concatenate(zi_stack, axis=0)
    lane16 = lax.broadcasted_iota(jnp.int32, (h, LANES), 1)
    g_f = _dot_hi(jnp.where(lane16 < p, bbr, 0.0), zr_all) - _dot_hi(jnp.where(lane16 < p, bbi, 0.0), zi_all)
    g_b = _dot_hi(jnp.where(lane16 < p, 0.0, bbr), zr_all) - _dot_hi(jnp.where(lane16 < p, 0.0, bbi), zi_all)

    lane = lax.broadcasted_iota(jnp.int32, (h, width), 1)
    row = lax.broadcasted_iota(jnp.int32, (h, width), 0)
    dcol = d_ref[0]
    for s in range(c):
        m_f = jnp.where(lane >= h * s, g_f if s == 0 else pltpu.roll(g_f, h * s, 1), 0.0)
        m_b = jnp.where(lane < h * (s + 1), g_b if s == c - 1 else pltpu.roll(g_b, h * (s + 1), 1), 0.0)
        skip = jnp.where(lane == h * s + row, dcol, 0.0)
        w1_ref[0, s * h:(s + 1) * h, 0:width] = (m_f + m_b + skip).astype(BF16)

    r8 = lax.broadcasted_iota(jnp.int32, (8, LANES), 0)
    a_ref[0] = jnp.where(r8 == 0, pwr[c:c + 1], jnp.where(r8 == 1, pwi[c:c + 1], 0.0))


def s5_prep(lam_re, lam_im, log_dt, b_re, b_im, c_re, c_im, d):
    _, g, p = lam_re.shape
    h = S5_GROUP
    width = CHUNK * h
    pack = lambda a: jnp.concatenate([a[0], a[1]], axis=-1)
    lr = pack(lam_re)[:, None, :]
    li = pack(lam_im)[:, None, :]
    ldt = pack(jnp.broadcast_to(log_dt[..., None], (2, g, p)))[:, None, :]
    btr = pack(jnp.swapaxes(b_re, -1, -2))
    bti = pack(jnp.swapaxes(b_im, -1, -2))
    ctr = jnp.tile(jnp.swapaxes(c_re, -1, -2), (1, 1, 1, CHUNK))
    cti = jnp.tile(jnp.swapaxes(c_im, -1, -2), (1, 1, 1, CHUNK))
    row = lambda last: pl.BlockSpec((1, 1, last), lambda i: (i, 0, 0))
    col = lambda rows, last: pl.BlockSpec((2, 1, rows, last), lambda i: (0, i, 0, 0))
    return pl.pallas_call(
        _s5_prep_kernel,
        grid=(g,),
        in_specs=[row(LANES), row(LANES), row(LANES),
                  pl.BlockSpec((1, h, LANES), lambda i: (i, 0, 0)), pl.BlockSpec((1, h, LANES), lambda i: (i, 0, 0)),
                  col(p, 1), col(p, 1), col(1, 1), col(p, width), col(p, width),
                  pl.BlockSpec((1, h, 1), lambda i: (i, 0, 0))],
        out_specs=[pl.BlockSpec((1, width, 2 * width), lambda i: (i, 0, 0)),
                   pl.BlockSpec((1, width, width), lambda i: (i, 0, 0)),
                   pl.BlockSpec((1, 8, LANES), lambda i: (i, 0, 0))],
        out_shape=[jax.ShapeDtypeStruct((g, width, 2 * width), BF16),
                   jax.ShapeDtypeStruct((g, width, width), BF16),
                   jax.ShapeDtypeStruct((g, 8, LANES), F32)],
        compiler_params=_params("parallel"),
        name="s5_prep",
    )(lr, li, ldt, btr, bti, lam_re[..., None], lam_im[..., None], log_dt[..., None, None], ctr, cti,
      d.reshape(g, h, 1))


def _chunk_permutation():
    n, half = GROUPS_PER_LANE_BLOCK, CHUNK // 2
    pm = np.zeros((half * LANES, n * half * S5_GROUP), np.float32)
    for t in range(half):
        for g in range(n):
            for h in range(S5_GROUP):
                pm[t * LANES + g * S5_GROUP + h, g * LANES + t * S5_GROUP + h] = 1.0
    return pm


def _to_chunks_kernel(x_ref, p_ref, o_ref):
    cb, _, bsz, _ = x_ref.shape
    rows, half = cb * bsz, CHUNK // 2
    for hf in range(2):
        x8 = jnp.concatenate([x_ref[:, hf * half + t].reshape(rows, LANES) for t in range(half)], axis=1)
        out = _dot(x8, p_ref[...])
        for g in range(GROUPS_PER_LANE_BLOCK):
            o_ref[g, :, hf * LANES:(hf + 1) * LANES] = out[:, g * LANES:(g + 1) * LANES].astype(BF16)


def _from_chunks_kernel(y_ref, p_ref, o_ref):
    cb, _, bsz, _ = o_ref.shape
    half = CHUNK // 2
    for hf in range(2):
        y8 = jnp.concatenate([y_ref[g, :, hf * LANES:(hf + 1) * LANES] for g in range(GROUPS_PER_LANE_BLOCK)], axis=1)
        out = _dot_nt(y8, p_ref[...])
        for t in range(half):
            o_ref[:, hf * half + t] = out[:, t * LANES:(t + 1) * LANES].astype(BF16).reshape(cb, bsz, LANES)


def _chunk_relayout(kern, x, perm, out_struct, cb, to_chunks):
    n = GROUPS_PER_LANE_BLOCK
    if to_chunks:
        nc, _, bsz, d = x.shape
    else:
        nc, _, bsz, d = out_struct.shape
    tm_spec = pl.BlockSpec((cb, CHUNK, bsz, LANES), lambda i, j: (i, 0, 0, j))
    ch_spec = pl.BlockSpec((n, cb * bsz, CHUNK * S5_GROUP), lambda i, j: (j, i, 0))
    return pl.pallas_call(
        kern,
        grid=(nc // cb, d // LANES),
        in_specs=[tm_spec if to_chunks else ch_spec, _resident(perm.shape)],
        out_specs=ch_spec if to_chunks else tm_spec,
        out_shape=out_struct,
        compiler_params=_params("parallel", "parallel"),
        name="to_chunks" if to_chunks else "from_chunks",
    )(x, perm)


def _s5_chunk_kernel(u_ref, w1_ref, w2_ref, a_ref, y_ref, yint_ref, sin_ref, st_ref, *, rows_per_chunk):
    gb, nrows, width = u_ref.shape
    rb = rows_per_chunk
    n_chunks = nrows // rb
    ns = S5_STATE
    fwd = lax.broadcasted_iota(jnp.int32, (rb, LANES), 1) < ns

    for g in range(gb):
        r = _dot(u_ref[g], w1_ref[g])
        yint_ref[...] = r[:, :width]
        sin_ref[...] = r[:, width:]
        a_re = a_ref[g, 0:1, :]
        a_im = a_ref[g, 1:2, :]

        def step(i, carry):
            s_re, s_im = carry
            rf = pl.ds(pl.multiple_of(i * rb, rb), rb)
            rr = pl.ds(pl.multiple_of((n_chunks - 1 - i) * rb, rb), rb)
            st_ref[rf, 0:ns] = s_re[:, 0:ns]
            st_ref[rf, LANES:LANES + ns] = s_im[:, 0:ns]
            st_ref[rr, ns:LANES] = s_re[:, ns:LANES]
            st_ref[rr, LANES + ns:2 * LANES] = s_im[:, ns:LANES]
            in_re = jnp.where(fwd, sin_ref[rf, 0:LANES], sin_ref[rr, 0:LANES])
            in_im = jnp.where(fwd, sin_ref[rf, LANES:2 * LANES], sin_ref[rr, LANES:2 * LANES])
            return (a_re * s_re - a_im * s_im + in_re, a_re * s_im + a_im * s_re + in_im)

        zero = jnp.zeros((rb, LANES), F32)
        lax.fori_loop(0, n_chunks, step, (zero, zero), unroll=4)
        y_ref[g] = (yint_ref[...] + _dot(st_ref[...].astype(BF16), w2_ref[g])).astype(BF16)


def s5_chunk(u2, w1, w2, a, rows_per_chunk, gb):
    g, nrows, width = u2.shape
    kern = functools.partial(_s5_chunk_kernel, rows_per_chunk=rows_per_chunk)
    return pl.pallas_call(
        kern,
        grid=(g // gb,),
        in_specs=[pl.BlockSpec((gb, nrows, width), lambda i: (i, 0, 0)),
                  pl.BlockSpec((gb, width, 2 * width), lambda i: (i, 0, 0)),
                  pl.BlockSpec((gb, width, width), lambda i: (i, 0, 0)),
                  pl.BlockSpec((gb, 8, LANES), lambda i: (i, 0, 0))],
        out_specs=pl.BlockSpec((gb, nrows, width), lambda i: (i, 0, 0)),
        out_shape=jax.ShapeDtypeStruct((g, nrows, width), BF16),
        scratch_shapes=[pltpu.VMEM((nrows, width), F32), pltpu.VMEM((nrows, width), F32),
                        pltpu.VMEM((nrows, width), F32)],
        compiler_params=_params("parallel"),
        name="s5_chunk",
    )(u2, w1, w2, a)


def _diff_lambda(lq1_ref, lk1_ref, lq2_ref, lk2_ref, lambda_init):
    return (jnp.exp(jnp.sum(lq1_ref[...] * lk1_ref[...], axis=-1, keepdims=True))
            - jnp.exp(jnp.sum(lq2_ref[...] * lk2_ref[...], axis=-1, keepdims=True)) + lambda_init)


def _map_queries(q):
    lane = lax.broadcasted_iota(jnp.int32, q.shape, 1)
    zero = jnp.zeros_like(q)
    return jnp.where(lane < DIFF_DH, q, zero), jnp.where(lane < DIFF_DH, zero, q)


def _sublayer_norm_store(o, sg_ref, o_ref, lambda_init):
    on = o * lax.rsqrt(jnp.mean(o * o, axis=-1, keepdims=True) + EPS) * sg_ref[...]
    o_ref[0] = (on * (1.0 - lambda_init)).astype(BF16)


def _diff_attn_stream_kernel(q_ref, k_ref, v_ref, lq1_ref, lk1_ref, lq2_ref, lk2_ref, sg_ref, o_ref, e_ref, *,
                             lambda_init, tk, sub):
    tq = q_ref.shape[1]
    seq = k_ref.shape[1]
    lam = _diff_lambda(lq1_ref, lk1_ref, lq2_ref, lk2_ref, lambda_init)
    n_sub = tq // sub

    sums = []
    for h in range(n_sub):
        qmaps = _map_queries(q_ref[0, h * sub:(h + 1) * sub, :])
        sums.append([])
        for mp in (0, 1):
            part = jnp.zeros((sub, LANES), F32)
            for kb in range(seq // tk):
                e = jnp.exp2(_dot_nt(qmaps[mp], k_ref[0, kb * tk:(kb + 1) * tk, :]))
                for c in range(tk // LANES):
                    part = part + e[:, c * LANES:(c + 1) * LANES]
                e_ref[h, mp, :, kb * tk:(kb + 1) * tk] = e.astype(BF16)
            sums[h].append(jnp.sum(part, axis=-1, keepdims=True))

    for h in range(n_sub):
        l1, l2 = sums[h]
        mu = jnp.broadcast_to(lam * l1 / l2, (sub, LANES)).astype(BF16)
        acc = None
        for kb in range(seq // tk):
            cols = []
            for c in range(tk // LANES):
                sl = slice(kb * tk + c * LANES, kb * tk + (c + 1) * LANES)
                cols.append(e_ref[h, 0, :, sl] - mu * e_ref[h, 1, :, sl])
            term = _dot(jnp.concatenate(cols, axis=1), v_ref[0, kb * tk:(kb + 1) * tk, :])
            acc = term if acc is None else acc + term
        o = acc * (1.0 / l1)
        on = o * lax.rsqrt(jnp.mean(o * o, axis=-1, keepdims=True) + EPS) * sg_ref[...]
        o_ref[0, h * sub:(h + 1) * sub, :] = (on * (1.0 - lambda_init)).astype(BF16)


def _diff_attn_exact_kernel(q_ref, k_ref, v_ref, lq1_ref, lk1_ref, lq2_ref, lk2_ref, sg_ref, o_ref, *, lambda_init):
    lam = _diff_lambda(lq1_ref, lk1_ref, lq2_ref, lk2_ref, lambda_init)
    k = k_ref[0]

    def softmax_parts(s):
        e = jnp.exp2(s - jnp.max(s, axis=-1, keepdims=True))
        return e, 1.0 / jnp.sum(e, axis=-1, keepdims=True)

    q1, q2 = _map_queries(q_ref[0])
    e1, r1 = softmax_parts(_dot_nt(q1, k))
    e2, r2 = softmax_parts(_dot_nt(q2, k))
    a = e1 * r1 - e2 * (r2 * lam)
    _sublayer_norm_store(_dot(a.astype(BF16), v_ref[0]), sg_ref, o_ref, lambda_init)


def diff_attn(q, k, v, lq1, lk1, lq2, lk2, sub_g, lambda_init, tq, streaming):
    b, seq, d = q.shape
    hw = 2 * DIFF_DH
    small = lambda n: pl.BlockSpec((1, n), lambda i, h, j: (0, 0))
    scratch = []
    if streaming:
        sub = tq
        tq = min(ATTN_SUBTILES * sub, seq)
        kern = functools.partial(_diff_attn_stream_kernel, lambda_init=lambda_init, tk=min(ATTN_KEY_BLOCK, seq), sub=sub)
        scratch = [pltpu.VMEM((tq // sub, 2, sub, seq), BF16)]
    else:
        kern = functools.partial(_diff_attn_exact_kernel, lambda_init=lambda_init)
    return pl.pallas_call(
        kern,
        grid=(b, DIFF_HEADS, seq // tq),
        in_specs=[pl.BlockSpec((1, tq, hw), lambda i, h, j: (i, j, h)),
                  pl.BlockSpec((1, seq, hw), lambda i, h, j: (i, 0, h)),
                  pl.BlockSpec((1, seq, hw), lambda i, h, j: (i, 0, h)),
                  small(DIFF_DH), small(DIFF_DH), small(DIFF_DH), small(DIFF_DH), small(hw)],
        out_specs=pl.BlockSpec((1, tq, hw), lambda i, h, j: (i, j, h)),
        out_shape=jax.ShapeDtypeStruct((b, seq, d), BF16),
        scratch_shapes=scratch,
        compiler_params=_params("parallel", "parallel", "parallel"),
        name="diff_attn_stream" if streaming else "diff_attn_exact",
    )(q, k, v, lq1, lk1, lq2, lk2, sub_g)


def _mem_kv_kernel(m_ref, g_ref, w_ref, kg_ref, k_ref, v_ref):
    d = m_ref.shape[-1]
    m = m_ref[0]
    mn = (m * lax.rsqrt(jnp.mean(m * m, axis=-1, keepdims=True) + EPS) * g_ref[...]).astype(BF16)
    kv = _dot(mn, w_ref[...])
    mdh = d // MEM_HEADS
    for h in range(MEM_HEADS):
        seg = kv[:, h * mdh:(h + 1) * mdh]
        segn = seg * lax.rsqrt(jnp.mean(seg * seg, axis=-1, keepdims=True) + EPS) * kg_ref[:, h * mdh:(h + 1) * mdh]
        k_ref[0, :, h * mdh:(h + 1) * mdh] = segn.astype(BF16)
    v_ref[0] = kv[:, d:].astype(BF16)


def mem_kv(mem, g, w_kv, kg):
    b, m, d = mem.shape
    blk = pl.BlockSpec((1, m, d), lambda i: (i, 0, 0))
    return pl.pallas_call(
        _mem_kv_kernel,
        grid=(b,),
        in_specs=[blk, _resident((1, d)), _resident((d, 2 * d)), _resident((1, d))],
        out_specs=[blk, blk],
        out_shape=[jax.ShapeDtypeStruct((b, m, d), BF16)] * 2,
        compiler_params=_params("parallel"),
        name="mem_kv",
    )(mem, g, w_kv, kg)


def _merge_kernel(x_ref, y_ref, do_ref, qm_ref, gt_ref, mk_ref, mv_ref, wglu_ref, wb_ref, wo_ref, o_ref):
    d = x_ref.shape[-1]
    z = jax.nn.gelu(y_ref[0].astype(F32))
    s5o = (z * jax.nn.sigmoid(_dot(z.astype(BF16), wglu_ref[...]))).astype(BF16)

    qm = qm_ref[0]
    mk = mk_ref[0]
    mv = mv_ref[0]
    mdh = d // MEM_HEADS
    mem_parts = []
    for h in range(MEM_HEADS):
        sl = slice(h * mdh, (h + 1) * mdh)
        s = _dot_nt(qm[:, sl], mk[:, sl])
        e = jnp.exp(s - jnp.max(s, axis=-1, keepdims=True))
        pr = e * (1.0 / jnp.sum(e, axis=-1, keepdims=True))
        mem_parts.append(_dot(pr.astype(BF16), mv[:, sl]).astype(BF16))
    mo = jnp.concatenate(mem_parts, axis=-1)

    merged = None
    for n, br in enumerate((s5o, do_ref[0], mo)):
        term = gt_ref[0, :, n * d:(n + 1) * d].astype(F32) * _dot(br, wb_ref[n])
        merged = term if merged is None else merged + term
    o_ref[0] = x_ref[0] + _dot(merged.astype(BF16), wo_ref[...])


def merge(x, y, do, qm, gates, mk, mv, w_glu, w_branch, w_out, tt):
    b, seq, d = x.shape
    m = mk.shape[1]
    tok = lambda width: pl.BlockSpec((1, tt, width), lambda i, j: (i, j, 0))
    memb = pl.BlockSpec((1, m, d), lambda i, j: (i, 0, 0))
    return pl.pallas_call(
        _merge_kernel,
        grid=(b, seq // tt),
        in_specs=[tok(d), tok(d), tok(d), tok(d), tok(3 * d), memb, memb,
                  _resident((d, d)), _resident((3, d, d)), _resident((d, d))],
        out_specs=tok(d),
        out_shape=jax.ShapeDtypeStruct((b, seq, d), F32),
        compiler_params=_params("parallel", "parallel"),
        name="merge",
    )(x, y, do, qm, gates, mk, mv, w_glu, w_branch, w_out)


def _ffn_kernel(x_ref, g_ref, wgu_ref, wd_ref, o_ref):
    x = x_ref[0]
    dff = wd_ref.shape[0]
    hn = (x * lax.rsqrt(jnp.mean(x * x, axis=-1, keepdims=True) + EPS) * g_ref[...]).astype(BF16)
    gate = _dot(hn, wgu_ref[:, :dff])
    up = _dot(hn, wgu_ref[:, dff:])
    act = (jax.nn.silu(gate) * up).astype(BF16)
    o_ref[0] = x + _dot(act, wd_ref[...])


def ffn(x, g, w_gate_up, w_down, tt):
    b, seq, d = x.shape
    dff = w_down.shape[0]
    tok = pl.BlockSpec((1, tt, d), lambda i, j: (i, j, 0))
    return pl.pallas_call(
        _ffn_kernel,
        grid=(b, seq // tt),
        in_specs=[tok, _resident((1, d)), _resident((d, 2 * dff)), _resident((dff, d))],
        out_specs=tok,
        out_shape=jax.ShapeDtypeStruct((b, seq, d), F32),
        compiler_params=_params("parallel", "parallel"),
        name="ffn",
    )(x, g, w_gate_up, w_down)


def _s5_mixer(u, s5_ops, perm):
    b, seq, d = u.shape
    nc = seq // CHUNK
    cb = min(S5_CHUNKS_PER_STEP, nc)
    w1, w2, a = s5_ops
    u_tm = jnp.swapaxes(u, 0, 1).reshape(nc, CHUNK, b, d)
    chunk_rows = jax.ShapeDtypeStruct((d // S5_GROUP, nc * b, CHUNK * S5_GROUP), BF16)
    u2 = _chunk_relayout(_to_chunks_kernel, u_tm, perm, chunk_rows, cb, True)
    y2 = s5_chunk(u2, w1, w2, a, b, S5_GROUPS_PER_STEP)
    y_tm = _chunk_relayout(_from_chunks_kernel, y2, perm, jax.ShapeDtypeStruct(u_tm.shape, BF16), cb, False)
    return jnp.swapaxes(y_tm.reshape(seq, b, d), 0, 1)


def _layer(x, mem, p, li, s5_ops, tables, perm):
    b, seq, d = x.shape
    tt = min(DENSE_TOKENS, seq)
    tq = min(ATTN_QUERY_ROWS, seq)
    row = lambda a: a.reshape(1, -1)
    tile_to_d = lambda a: jnp.tile(a, d // a.shape[0]).reshape(1, d)
    head_of_lane = (lax.broadcasted_iota(jnp.int32, (d, LANES), 0) // DIFF_DH
                    == lax.broadcasted_iota(jnp.int32, (d, LANES), 1))
    hsel = jnp.where(head_of_lane, 1.0 / DIFF_DH, 0.0).astype(BF16)
    hexp = head_of_lane.T.astype(BF16)
    cos, sin = tables

    u, q, k, v, qm, gates = proj(x, row(p["norm_mix_g"][li]), p["w_in"][li].astype(BF16), hsel, hexp, row(p["b_gate"][li]),
                                 tile_to_d(p["diff_q_g"][li]), tile_to_d(p["diff_k_g"][li]), tile_to_d(p["mem_q_g"][li]),
                                 cos[:seq], sin[:seq], tt)

    y = _s5_mixer(u, s5_ops, perm)

    lambda_init = 0.8 - 0.6 * math.exp(-0.3 * li)
    attn_args = (q, k, v, row(p["diff_lq1"][li]), row(p["diff_lk1"][li]), row(p["diff_lq2"][li]),
                 row(p["diff_lk2"][li]), row(p["diff_sub_g"][li]))
    score_bound = (1.05 * DIFF_DH ** 0.5 * LOG2E) * jnp.max(jnp.abs(p["diff_q_g"][li])) * jnp.max(jnp.abs(p["diff_k_g"][li]))
    do = lax.cond(score_bound <= SCORE_BOUND,
                  lambda *a: diff_attn(*a, lambda_init, tq, True),
                  lambda *a: diff_attn(*a, lambda_init, tq, False), *attn_args)

    mk, mv = mem_kv(mem, row(p["mem_norm_g"][li]), p["w_mem_kv"][li].astype(BF16), tile_to_d(p["mem_k_g"][li]))
    x1 = merge(x, y, do, qm, gates, mk, mv, p["s5_w_glu"][li].astype(BF16), p["w_branch"][li].astype(BF16),
               p["w_out"][li].astype(BF16), tt)
    return ffn(x1, row(p["ffn_norm_g"][li]), p["w_gate_up"][li].astype(BF16), p["w_down"][li].astype(BF16), tt)


def kernel(x_prompt, x_sample, mem_prompt, mem_sample, norm_mix_g, w_in, b_gate, s5_lam_re, s5_lam_im, s5_log_dt, s5_b_re, s5_b_im, s5_c_re, s5_c_im, s5_d, s5_w_glu, diff_q_g, diff_k_g, diff_lq1, diff_lk1, diff_lq2, diff_lk2, diff_sub_g, mem_norm_g, w_mem_kv, mem_q_g, mem_k_g, w_branch, w_out, ffn_norm_g, w_gate_up, w_down):
    p = dict(norm_mix_g=norm_mix_g, w_in=w_in, b_gate=b_gate, s5_w_glu=s5_w_glu, diff_q_g=diff_q_g, diff_k_g=diff_k_g,
             diff_lq1=diff_lq1, diff_lk1=diff_lk1, diff_lq2=diff_lq2, diff_lk2=diff_lk2, diff_sub_g=diff_sub_g,
             mem_norm_g=mem_norm_g, w_mem_kv=w_mem_kv, mem_q_g=mem_q_g, mem_k_g=mem_k_g, w_branch=w_branch,
             w_out=w_out, ffn_norm_g=ffn_norm_g, w_gate_up=w_gate_up, w_down=w_down)
    tables = rope_tables(max(x_prompt.shape[1], x_sample.shape[1]))
    perm = jnp.asarray(_chunk_permutation(), BF16)
    y_prompt, y_sample = x_prompt, x_sample
    for li in range(norm_mix_g.shape[0]):
        s5_ops = s5_prep(s5_lam_re[li], s5_lam_im[li], s5_log_dt[li], s5_b_re[li], s5_b_im[li],
                         s5_c_re[li], s5_c_im[li], s5_d[li])
        y_prompt = _layer(y_prompt, mem_prompt, p, li, s5_ops, tables, perm)
        y_sample = _layer(y_sample, mem_sample, p, li, s5_ops, tables, perm)
    return (y_prompt, y_sample)
```

```python
import functools
import math

import numpy as np
import jax
import jax.numpy as jnp
from jax import lax
from jax.experimental import pallas as pl
from jax.experimental.pallas import tpu as pltpu

F32 = jnp.float32
BF16 = jnp.bfloat16
EPS = 1e-6
ROPE_THETA = 10000.0
LOG2E = math.log2(math.e)

S5_GROUP = 16
S5_STATE = 64
CHUNK = 16
DIFF_HEADS = 8
DIFF_DH = 64
MEM_HEADS = 4
LANES = 128
GROUPS_PER_LANE_BLOCK = LANES // S5_GROUP
VMEM_LIMIT = 56 * 1024 * 1024
DENSE_TOKENS = 512
ATTN_QUERY_ROWS = 256
ATTN_SUBTILES = 8
ATTN_KEY_BLOCK = 512
S5_CHUNKS_PER_STEP = 32
S5_GROUPS_PER_STEP = 2
SCORE_BOUND = 60.0

HIGHEST = lax.Precision.HIGHEST


def _dot(a, b):
    return jnp.dot(a, b, preferred_element_type=F32)


def _dot_nt(a, b):
    return lax.dot_general(a, b, (((1,), (1,)), ((), ())), preferred_element_type=F32)


def _dot_hi(a, b):
    return jnp.dot(a, b, precision=HIGHEST, preferred_element_type=F32)


def _params(*sem):
    return pltpu.CompilerParams(dimension_semantics=sem, vmem_limit_bytes=VMEM_LIMIT)


def _resident(shape):
    nd = len(shape)
    return pl.BlockSpec(shape, lambda *_: (0,) * nd, pipeline_mode=pl.Buffered(1))


def _rope_kernel(cos_ref, sin_ref):
    rows = cos_ref.shape[0]
    base = pl.program_id(0) * rows
    pos = (lax.broadcasted_iota(jnp.int32, (rows, LANES), 0) + base).astype(F32)
    lane = lax.broadcasted_iota(jnp.int32, (rows, LANES), 1)
    half = DIFF_DH // 2
    freq = (lane % half).astype(F32)
    inv = jnp.exp(freq * (-math.log(ROPE_THETA) / half))
    ang = pos * inv
    cos_ref[...] = jnp.cos(ang)
    s = jnp.sin(ang)
    sin_ref[...] = jnp.where((lane % DIFF_DH) < half, -s, s)


def rope_tables(seq):
    rows = min(seq, 512)
    return pl.pallas_call(
        _rope_kernel,
        grid=(seq // rows,),
        out_specs=[pl.BlockSpec((rows, LANES), lambda i: (i, 0))] * 2,
        out_shape=[jax.ShapeDtypeStruct((seq, LANES), F32)] * 2,
        name="rope_tables",
    )()


def _proj_kernel(x_ref, g_ref, w_ref, hsel_ref, hexp_ref, bg_ref, qg_ref, kg_ref, mqg_ref, cos_ref, sin_ref,
                 u_ref, q_ref, k_ref, v_ref, qm_ref, gt_ref):
    d = x_ref.shape[-1]
    x = x_ref[0]
    hn = (x * lax.rsqrt(jnp.mean(x * x, axis=-1, keepdims=True) + EPS) * g_ref[...]).astype(BF16)

    def mm(j):
        return _dot(hn, w_ref[:, j * d:(j + 1) * d])

    u_ref[0] = mm(0).astype(BF16)

    cos = cos_ref[...]
    sin = sin_ref[...]
    lane = lax.broadcasted_iota(jnp.int32, cos.shape, 1)
    first_half = (lane % DIFF_DH) < (DIFF_DH // 2)

    def qk_norm_rope(t, gain, scale, out_ref):
        ms = _dot(_dot((t * t).astype(BF16), hsel_ref[...]).astype(BF16), hexp_ref[...])
        tn = t * lax.rsqrt(ms + EPS) * gain
        for j in range(d // LANES):
            blk = tn[:, j * LANES:(j + 1) * LANES]
            partner = jnp.where(first_half, pltpu.roll(blk, LANES - DIFF_DH // 2, 1), pltpu.roll(blk, DIFF_DH // 2, 1))
            out_ref[0, :, j * LANES:(j + 1) * LANES] = ((blk * cos + partner * sin) * scale).astype(BF16)

    qk_norm_rope(mm(1), qg_ref[...], DIFF_DH ** -0.5 * LOG2E, q_ref)
    qk_norm_rope(mm(2), kg_ref[...], 1.0, k_ref)
    v_ref[0] = mm(3).astype(BF16)

    qm = mm(4)
    mdh = d // MEM_HEADS
    for h in range(MEM_HEADS):
        seg = qm[:, h * mdh:(h + 1) * mdh]
        segn = seg * lax.rsqrt(jnp.mean(seg * seg, axis=-1, keepdims=True) + EPS) * mqg_ref[:, h * mdh:(h + 1) * mdh]
        qm_ref[0, :, h * mdh:(h + 1) * mdh] = (segn * (mdh ** -0.5)).astype(BF16)

    for j in range(3):
        gl = mm(5 + j) + bg_ref[:, j * d:(j + 1) * d]
        gt_ref[0, :, j * d:(j + 1) * d] = jax.nn.sigmoid(gl).astype(BF16)


def proj(x, g, w_in, hsel, hexp, b_gate, qg, kg, mqg, cos, sin, tt):
    b, seq, d = x.shape
    ncol = w_in.shape[1]
    tok = lambda width: pl.BlockSpec((1, tt, width), lambda i, j: (i, j, 0))
    outs = [d, d, d, d, d, 3 * d]
    return pl.pallas_call(
        _proj_kernel,
        grid=(b, seq // tt),
        in_specs=[tok(d), _resident((1, d)), _resident((d, ncol)), _resident((d, LANES)), _resident((LANES, d)),
                  _resident((1, 3 * d)),
                  _resident((1, d)), _resident((1, d)), _resident((1, d)),
                  pl.BlockSpec((tt, LANES), lambda i, j: (j, 0)), pl.BlockSpec((tt, LANES), lambda i, j: (j, 0))],
        out_specs=[tok(w) for w in outs],
        out_shape=[jax.ShapeDtypeStruct((b, seq, w), BF16) for w in outs],
        compiler_params=_params("parallel", "parallel"),
        name="proj",
    )(x, g, w_in, hsel, hexp, b_gate, qg, kg, mqg, cos, sin)


def _s5_prep_kernel(lr_ref, li_ref, ldt_ref, btr_ref, bti_ref, lcr_ref, lci_ref, ldtc_ref, ctr_ref, cti_ref, d_ref,
                    w1_ref, w2_ref, a_ref):
    p, h, c = S5_STATE, S5_GROUP, CHUNK
    width = c * h
    lane128 = lax.broadcasted_iota(jnp.int32, (1, LANES), 1)
    fwd128 = lane128 < p

    dt = jnp.exp(ldt_ref[0])
    zr, zi = lr_ref[0] * dt, li_ref[0] * dt
    kk = jnp.minimum(lax.broadcasted_iota(jnp.int32, (24, 1), 0), c).astype(F32)
    mag = jnp.exp(kk * zr)
    pwr, pwi = mag * jnp.cos(kk * zi), mag * jnp.sin(kk * zi)

    lr, li = lr_ref[0], li_ref[0]
    nr, ni = pwr[1:2] - 1.0, pwi[1:2]
    den = lr * lr + li * li
    cr, ci = (nr * lr + ni * li) / den, (ni * lr - nr * li) / den
    btr, bti = btr_ref[0], bti_ref[0]
    bbr, bbi = cr * btr - ci * bti, cr * bti + ci * btr

    for s in range(c):
        sr = jnp.where(fwd128, pwr[c - 1 - s:c - s], pwr[s:s + 1])
        si = jnp.where(fwd128, pwi[c - 1 - s:c - s], pwi[s:s + 1])
        w1_ref[0, s * h:(s + 1) * h, width:width + LANES] = (bbr * sr - bbi * si).astype(BF16)
        w1_ref[0, s * h:(s + 1) * h, width + LANES:width + 2 * LANES] = (bbr * si + bbi * sr).astype(BF16)

    krow = jnp.minimum(lax.broadcasted_iota(jnp.int32, (1, LANES), 1), c).astype(F32)
    sel_k = lax.broadcasted_iota(jnp.int32, (LANES, width), 0)
    sel_t = lax.broadcasted_iota(jnp.int32, (LANES, width), 1) // h

    def expand(pw, power_of_t):
        return _dot_hi(pw, (sel_k == power_of_t(sel_t)).astype(F32))

    zr_stack, zi_stack = [], []
    for dr in range(2):
        dtc = jnp.exp(ldtc_ref[dr, 0])
        zrc, zic = lcr_ref[dr, 0] * dtc, lci_ref[dr, 0] * dtc
        magc = jnp.exp(zrc * krow)
        pcr, pci = magc * jnp.cos(zic * krow), magc * jnp.sin(zic * krow)
        ctr, cti = ctr_ref[dr, 0], cti_ref[dr, 0]
        lag = (lambda t: t) if dr == 0 else (lambda t: c - 1 - t)
        er, ei = expand(pcr, lag), expand(pci, lag)
        zr_stack.append(er * ctr - ei * cti)
        zi_stack.append(er * cti + ei * ctr)
        rd = (lambda t: t + 1) if dr == 0 else (lambda t: c - t)
        er, ei = expand(pcr, rd), expand(pci, rd)
        re_rows = (er * ctr - ei * cti).astype(BF16)
        im_rows = (-(er * cti + ei * ctr)).astype(BF16)
        w2_ref[0, dr * p:(dr + 1) * p] = re_rows
        w2_ref[0, LANES + dr * p:LANES + (dr + 1) * p] = im_rows

    zr_all = jnp.concatenate(zr_stack, axis=0)
    zi_all = jnp.concatenate(zi_stack, axis=0)
    lane16 = lax.broadcasted_iota(jnp.int32, (h, LANES), 1)
    g_f = _dot_hi(jnp.where(lane16 < p, bbr, 0.0), zr_all) - _dot_hi(jnp.where(lane16 < p, bbi, 0.0), zi_all)
    g_b = _dot_hi(jnp.where(lane16 < p, 0.0, bbr), zr_all) - _dot_hi(jnp.where(lane16 < p, 0.0, bbi), zi_all)

    lane = lax.broadcasted_iota(jnp.int32, (h, width), 1)
    row = lax.broadcasted_iota(jnp.int32, (h, width), 0)
    dcol = d_ref[0]
    for s in range(c):
        m_f = jnp.where(lane >= h * s, g_f if s == 0 else pltpu.roll(g_f, h * s, 1), 0.0)
        m_b = jnp.where(lane < h * (s + 1), g_b if s == c - 1 else pltpu.roll(g_b, h * (s + 1), 1), 0.0)
        skip = jnp.where(lane == h * s + row, dcol, 0.0)
        w1_ref[0, s * h:(s + 1) * h, 0:width] = (m_f + m_b + skip).astype(BF16)

    r8 = lax.broadcasted_iota(jnp.int32, (8, LANES), 0)
    a_ref[0] = jnp.where(r8 == 0, pwr[c:c + 1], jnp.where(r8 == 1, pwi[c:c + 1], 0.0))


def s5_prep(lam_re, lam_im, log_dt, b_re, b_im, c_re, c_im, d):
    _, g, p = lam_re.shape
    h = S5_GROUP
    width = CHUNK * h
    pack = lambda a: jnp.concatenate([a[0], a[1]], axis=-1)
    lr = pack(lam_re)[:, None, :]
    li = pack(lam_im)[:, None, :]
    ldt = pack(jnp.broadcast_to(log_dt[..., None], (2, g, p)))[:, None, :]
    btr = pack(jnp.swapaxes(b_re, -1, -2))
    bti = pack(jnp.swapaxes(b_im, -1, -2))
    ctr = jnp.tile(jnp.swapaxes(c_re, -1, -2), (1, 1, 1, CHUNK))
    cti = jnp.tile(jnp.swapaxes(c_im, -1, -2), (1, 1, 1, CHUNK))
    row = lambda last: pl.BlockSpec((1, 1, last), lambda i: (i, 0, 0))
    col = lambda rows, last: pl.BlockSpec((2, 1, rows, last), lambda i: (0, i, 0, 0))
    return pl.pallas_call(
        _s5_prep_kernel,
        grid=(g,),
        in_specs=[row(LANES), row(LANES), row(LANES),
                  pl.BlockSpec((1, h, LANES), lambda i: (i, 0, 0)), pl.BlockSpec((1, h, LANES), lambda i: (i, 0, 0)),
                  col(p, 1), col(p, 1), col(1, 1), col(p, width), col(p, width),
                  pl.BlockSpec((1, h, 1), lambda i: (i, 0, 0))],
        out_specs=[pl.BlockSpec((1, width, 2 * width), lambda i: (i, 0, 0)),
                   pl.BlockSpec((1, width, width), lambda i: (i, 0, 0)),
                   pl.BlockSpec((1, 8, LANES), lambda i: (i, 0, 0))],
        out_shape=[jax.ShapeDtypeStruct((g, width, 2 * width), BF16),
                   jax.ShapeDtypeStruct((g, width, width), BF16),
                   jax.ShapeDtypeStruct((g, 8, LANES), F32)],
        compiler_params=_params("parallel"),
        name="s5_prep",
    )(lr, li, ldt, btr, bti, lam_re[..., None], lam_im[..., None], log_dt[..., None, None], ctr, cti,
      d.reshape(g, h, 1))


def _chunk_permutation():
    n, half = GROUPS_PER_LANE_BLOCK, CHUNK // 2
    pm = np.zeros((half * LANES, n * half * S5_GROUP), np.float32)
    for t in range(half):
        for g in range(n):
            for h in range(S5_GROUP):
                pm[t * LANES + g * S5_GROUP + h, g * LANES + t * S5_GROUP + h] = 1.0
    return pm


def _to_chunks_kernel(x_ref, p_ref, o_ref):
    cb, _, bsz, _ = x_ref.shape
    rows, half = cb * bsz, CHUNK // 2
    for hf in range(2):
        x8 = jnp.concatenate([x_ref[:, hf * half + t].reshape(rows, LANES) for t in range(half)], axis=1)
        out = _dot(x8, p_ref[...])
        for g in range(GROUPS_PER_LANE_BLOCK):
            o_ref[g, :, hf * LANES:(hf + 1) * LANES] = out[:, g * LANES:(g + 1) * LANES].astype(BF16)


def _from_chunks_kernel(y_ref, p_ref, o_ref):
    cb, _, bsz, _ = o_ref.shape
    half = CHUNK // 2
    for hf in range(2):
        y8 = jnp.concatenate([y_ref[g, :, hf * LANES:(hf + 1) * LANES] for g in range(GROUPS_PER_LANE_BLOCK)], axis=1)
        out = _dot_nt(y8, p_ref[...])
        for t in range(half):
            o_ref[:, hf * half + t] = out[:, t * LANES:(t + 1) * LANES].astype(BF16).reshape(cb, bsz, LANES)


def _chunk_relayout(kern, x, perm, out_struct, cb, to_chunks):
    n = GROUPS_PER_LANE_BLOCK
    if to_chunks:
        nc, _, bsz, d = x.shape
    else:
        nc, _, bsz, d = out_struct.shape
    tm_spec = pl.BlockSpec((cb, CHUNK, bsz, LANES), lambda i, j: (i, 0, 0, j))
    ch_spec = pl.BlockSpec((n, cb * bsz, CHUNK * S5_GROUP), lambda i, j: (j, i, 0))
    return pl.pallas_call(
        kern,
        grid=(nc // cb, d // LANES),
        in_specs=[tm_spec if to_chunks else ch_spec, _resident(perm.shape)],
        out_specs=ch_spec if to_chunks else tm_spec,
        out_shape=out_struct,
        compiler_params=_params("parallel", "parallel"),
        name="to_chunks" if to_chunks else "from_chunks",
    )(x, perm)


def _s5_chunk_kernel(u_ref, w1_ref, w2_ref, a_ref, y_ref, yint_ref, sin_ref, st_ref, *, rows_per_chunk):
    gb, nrows, width = u_ref.shape
    rb = rows_per_chunk
    n_chunks = nrows // rb
    ns = S5_STATE
    fwd = lax.broadcasted_iota(jnp.int32, (rb, LANES), 1) < ns

    for g in range(gb):
        r = _dot(u_ref[g], w1_ref[g])
        yint_ref[...] = r[:, :width]
        sin_ref[...] = r[:, width:]
        a_re = a_ref[g, 0:1, :]
        a_im = a_ref[g, 1:2, :]

        def step(i, carry):
            s_re, s_im = carry
            rf = pl.ds(pl.multiple_of(i * rb, rb), rb)
            rr = pl.ds(pl.multiple_of((n_chunks - 1 - i) * rb, rb), rb)
            st_ref[rf, 0:ns] = s_re[:, 0:ns]
            st_ref[rf, LANES:LANES + ns] = s_im[:, 0:ns]
            st_ref[rr, ns:LANES] = s_re[:, ns:LANES]
            st_ref[rr, LANES + ns:2 * LANES] = s_im[:, ns:LANES]
            in_re = jnp.where(fwd, sin_ref[rf, 0:LANES], sin_ref[rr, 0:LANES])
            in_im = jnp.where(fwd, sin_ref[rf, LANES:2 * LANES], sin_ref[rr, LANES:2 * LANES])
            return (a_re * s_re - a_im * s_im + in_re, a_re * s_im + a_im * s_re + in_im)

        zero = jnp.zeros((rb, LANES), F32)
        lax.fori_loop(0, n_chunks, step, (zero, zero), unroll=4)
        y_ref[g] = (yint_ref[...] + _dot(st_ref[...].astype(BF16), w2_ref[g])).astype(BF16)


def s5_chunk(u2, w1, w2, a, rows_per_chunk, gb):
    g, nrows, width = u2.shape
    kern = functools.partial(_s5_chunk_kernel, rows_per_chunk=rows_per_chunk)
    return pl.pallas_call(
        kern,
        grid=(g // gb,),
        in_specs=[pl.BlockSpec((gb, nrows, width), lambda i: (i, 0, 0)),
                  pl.BlockSpec((gb, width, 2 * width), lambda i: (i, 0, 0)),
                  pl.BlockSpec((gb, width, width), lambda i: (i, 0, 0)),
                  pl.BlockSpec((gb, 8, LANES), lambda i: (i, 0, 0))],
        out_specs=pl.BlockSpec((gb, nrows, width), lambda i: (i, 0, 0)),
        out_shape=jax.ShapeDtypeStruct((g, nrows, width), BF16),
        scratch_shapes=[pltpu.VMEM((nrows, width), F32), pltpu.VMEM((nrows, width), F32),
                        pltpu.VMEM((nrows, width), F32)],
        compiler_params=_params("parallel"),
        name="s5_chunk",
    )(u2, w1, w2, a)


def _diff_lambda(lq1_ref, lk1_ref, lq2_ref, lk2_ref, lambda_init):
    return (jnp.exp(jnp.sum(lq1_ref[...] * lk1_ref[...], axis=-1, keepdims=True))
            - jnp.exp(jnp.sum(lq2_ref[...] * lk2_ref[...], axis=-1, keepdims=True)) + lambda_init)


def _map_queries(q):
    lane = lax.broadcasted_iota(jnp.int32, q.shape, 1)
    zero = jnp.zeros_like(q)
    return jnp.where(lane < DIFF_DH, q, zero), jnp.where(lane < DIFF_DH, zero, q)


def _sublayer_norm_store(o, sg_ref, o_ref, lambda_init):
    on = o * lax.rsqrt(jnp.mean(o * o, axis=-1, keepdims=True) + EPS) * sg_ref[...]
    o_ref[0] = (on * (1.0 - lambda_init)).astype(BF16)


def _diff_attn_stream_kernel(q_ref, k_ref, v_ref, lq1_ref, lk1_ref, lq2_ref, lk2_ref, sg_ref, o_ref, e_ref, *,
                             lambda_init, tk, sub):
    tq = q_ref.shape[1]
    seq = k_ref.shape[1]
    lam = _diff_lambda(lq1_ref, lk1_ref, lq2_ref, lk2_ref, lambda_init)
    n_sub = tq // sub

    nkb = seq // tk

    def score_block(qmaps, h, kb, part):
        for mp in (0, 1):
            e = jnp.exp2(_dot_nt(qmaps[mp], k_ref[0, kb * tk:(kb + 1) * tk, :]))
            for c in range(tk // LANES):
                part[mp] = part[mp] + e[:, c * LANES:(c + 1) * LANES]
            e_ref[h % 2, mp, :, kb * tk:(kb + 1) * tk] = e.astype(BF16)

    def value_block(h, kb, mu, acc):
        cols = []
        for c in range(tk // LANES):
            sl = slice(kb * tk + c * LANES, kb * tk + (c + 1) * LANES)
            cols.append(e_ref[h % 2, 0, :, sl] - mu * e_ref[h % 2, 1, :, sl])
        term = _dot(jnp.concatenate(cols, axis=1), v_ref[0, kb * tk:(kb + 1) * tk, :])
        return term if acc is None else acc + term

    prev = None
    for h in range(n_sub + 1):
        if h < n_sub:
            qmaps = _map_queries(q_ref[0, h * sub:(h + 1) * sub, :])
            part = [jnp.zeros((sub, LANES), F32), jnp.zeros((sub, LANES), F32)]
        acc = None
        for kb in range(nkb):
            if h < n_sub:
                score_block(qmaps, h, kb, part)
            if prev is not None:
                acc = value_block(h - 1, kb, prev[1], acc)
        if prev is not None:
            o = acc * (1.0 / prev[0])
            on = o * lax.rsqrt(jnp.mean(o * o, axis=-1, keepdims=True) + EPS) * sg_ref[...]
            o_ref[0, (h - 1) * sub:h * sub, :] = (on * (1.0 - lambda_init)).astype(BF16)
        if h < n_sub:
            l1, l2 = [jnp.sum(pt, axis=-1, keepdims=True) for pt in part]
            prev = (l1, jnp.broadcast_to(lam * l1 / l2, (sub, LANES)).astype(BF16))


def _diff_attn_exact_kernel(q_ref, k_ref, v_ref, lq1_ref, lk1_ref, lq2_ref, lk2_ref, sg_ref, o_ref, *, lambda_init):
    lam = _diff_lambda(lq1_ref, lk1_ref, lq2_ref, lk2_ref, lambda_init)
    k = k_ref[0]

    def softmax_parts(s):
        e = jnp.exp2(s - jnp.max(s, axis=-1, keepdims=True))
        return e, 1.0 / jnp.sum(e, axis=-1, keepdims=True)

    q1, q2 = _map_queries(q_ref[0])
    e1, r1 = softmax_parts(_dot_nt(q1, k))
    e2, r2 = softmax_parts(_dot_nt(q2, k))
    a = e1 * r1 - e2 * (r2 * lam)
    _sublayer_norm_store(_dot(a.astype(BF16), v_ref[0]), sg_ref, o_ref, lambda_init)


def diff_attn(q, k, v, lq1, lk1, lq2, lk2, sub_g, lambda_init, tq, streaming):
    b, seq, d = q.shape
    hw = 2 * DIFF_DH
    small = lambda n: pl.BlockSpec((1, n), lambda i, h, j: (0, 0))
    scratch = []
    if streaming:
        sub = tq
        tq = min(ATTN_SUBTILES * sub, seq)
        kern = functools.partial(_diff_attn_stream_kernel, lambda_init=lambda_init, tk=min(ATTN_KEY_BLOCK, seq), sub=sub)
        scratch = [pltpu.VMEM((2, 2, sub, seq), BF16)]
    else:
        kern = functools.partial(_diff_attn_exact_kernel, lambda_init=lambda_init)
    return pl.pallas_call(
        kern,
        grid=(b, DIFF_HEADS, seq // tq),
        in_specs=[pl.BlockSpec((1, tq, hw), lambda i, h, j: (i, j, h)),
                  pl.BlockSpec((1, seq, hw), lambda i, h, j: (i, 0, h)),
                  pl.BlockSpec((1, seq, hw), lambda i, h, j: (i, 0, h)),
                  small(DIFF_DH), small(DIFF_DH), small(DIFF_DH), small(DIFF_DH), small(hw)],
        out_specs=pl.BlockSpec((1, tq, hw), lambda i, h, j: (i, j, h)),
        out_shape=jax.ShapeDtypeStruct((b, seq, d), BF16),
        scratch_shapes=scratch,
        compiler_params=_params("parallel", "parallel", "parallel"),
        name="diff_attn_stream" if streaming else "diff_attn_exact",
    )(q, k, v, lq1, lk1, lq2, lk2, sub_g)


def _mem_kv_kernel(m_ref, g_ref, w_ref, kg_ref, k_ref, v_ref):
    d = m_ref.shape[-1]
    m = m_ref[0]
    mn = (m * lax.rsqrt(jnp.mean(m * m, axis=-1, keepdims=True) + EPS) * g_ref[...]).astype(BF16)
    kv = _dot(mn, w_ref[...])
    mdh = d // MEM_HEADS
    for h in range(MEM_HEADS):
        seg = kv[:, h * mdh:(h + 1) * mdh]
        segn = seg * lax.rsqrt(jnp.mean(seg * seg, axis=-1, keepdims=True) + EPS) * kg_ref[:, h * mdh:(h + 1) * mdh]
        k_ref[0, :, h * mdh:(h + 1) * mdh] = segn.astype(BF16)
    v_ref[0] = kv[:, d:].astype(BF16)


def mem_kv(mem, g, w_kv, kg):
    b, m, d = mem.shape
    blk = pl.BlockSpec((1, m, d), lambda i: (i, 0, 0))
    return pl.pallas_call(
        _mem_kv_kernel,
        grid=(b,),
        in_specs=[blk, _resident((1, d)), _resident((d, 2 * d)), _resident((1, d))],
        out_specs=[blk, blk],
        out_shape=[jax.ShapeDtypeStruct((b, m, d), BF16)] * 2,
        compiler_params=_params("parallel"),
        name="mem_kv",
    )(mem, g, w_kv, kg)


def _merge_kernel(x_ref, y_ref, do_ref, qm_ref, gt_ref, mk_ref, mv_ref, wglu_ref, wb_ref, wo_ref, o_ref):
    d = x_ref.shape[-1]
    mdh = d // MEM_HEADS
    heads = [slice(h * mdh, (h + 1) * mdh) for h in range(MEM_HEADS)]

    def gated(n, br):
        return gt_ref[0, :, n * d:(n + 1) * d].astype(F32) * _dot(br, wb_ref[n])

    merged = gated(1, do_ref[0])
    qm = qm_ref[0]
    mk = mk_ref[0]
    scores = [_dot_nt(qm[:, sl], mk[:, sl]) for sl in heads]

    z = jax.nn.gelu(y_ref[0].astype(F32))
    s5o = (z * jax.nn.sigmoid(_dot(z.astype(BF16), wglu_ref[...]))).astype(BF16)

    mv = mv_ref[0]
    mem_parts = []
    for s, sl in zip(scores, heads):
        e = jnp.exp(s - jnp.max(s, axis=-1, keepdims=True))
        pr = e * (1.0 / jnp.sum(e, axis=-1, keepdims=True))
        mem_parts.append(_dot(pr.astype(BF16), mv[:, sl]).astype(BF16))
    mo = jnp.concatenate(mem_parts, axis=-1)

    merged = merged + gated(0, s5o) + gated(2, mo)
    o_ref[0] = x_ref[0] + _dot(merged.astype(BF16), wo_ref[...])


def merge(x, y, do, qm, gates, mk, mv, w_glu, w_branch, w_out, tt):
    b, seq, d = x.shape
    m = mk.shape[1]
    tok = lambda width: pl.BlockSpec((1, tt, width), lambda i, j: (i, j, 0))
    memb = pl.BlockSpec((1, m, d), lambda i, j: (i, 0, 0))
    return pl.pallas_call(
        _merge_kernel,
        grid=(b, seq // tt),
        in_specs=[tok(d), tok(d), tok(d), tok(d), tok(3 * d), memb, memb,
                  _resident((d, d)), _resident((3, d, d)), _resident((d, d))],
        out_specs=tok(d),
        out_shape=jax.ShapeDtypeStruct((b, seq, d), F32),
        compiler_params=_params("parallel", "parallel"),
        name="merge",
    )(x, y, do, qm, gates, mk, mv, w_glu, w_branch, w_out)


def _ffn_kernel(x_ref, g_ref, wgu_ref, wd_ref, o_ref):
    dff = wd_ref.shape[0]
    rows = x_ref.shape[1]
    half = rows // 2 if rows % 512 == 0 else rows
    for r0 in range(0, rows, half):
        x = x_ref[0, r0:r0 + half, :]
        hn = (x * lax.rsqrt(jnp.mean(x * x, axis=-1, keepdims=True) + EPS) * g_ref[...]).astype(BF16)
        gate = _dot(hn, wgu_ref[:, :dff])
        up = _dot(hn, wgu_ref[:, dff:])
        act = (jax.nn.silu(gate) * up).astype(BF16)
        o_ref[0, r0:r0 + half, :] = x + _dot(act, wd_ref[...])


def ffn(x, g, w_gate_up, w_down, tt):
    b, seq, d = x.shape
    dff = w_down.shape[0]
    tok = pl.BlockSpec((1, tt, d), lambda i, j: (i, j, 0))
    return pl.pallas_call(
        _ffn_kernel,
        grid=(b, seq // tt),
        in_specs=[tok, _resident((1, d)), _resident((d, 2 * dff)), _resident((dff, d))],
        out_specs=tok,
        out_shape=jax.ShapeDtypeStruct((b, seq, d), F32),
        compiler_params=_params("parallel", "parallel"),
        name="ffn",
    )(x, g, w_gate_up, w_down)


def _s5_mixer(u, s5_ops, perm):
    b, seq, d = u.shape
    nc = seq // CHUNK
    cb = min(S5_CHUNKS_PER_STEP, nc)
    w1, w2, a = s5_ops
    u_tm = jnp.swapaxes(u, 0, 1).reshape(nc, CHUNK, b, d)
    chunk_rows = jax.ShapeDtypeStruct((d // S5_GROUP, nc * b, CHUNK * S5_GROUP), BF16)
    u2 = _chunk_relayout(_to_chunks_kernel, u_tm, perm, chunk_rows, cb, True)
    y2 = s5_chunk(u2, w1, w2, a, b, S5_GROUPS_PER_STEP)
    y_tm = _chunk_relayout(_from_chunks_kernel, y2, perm, jax.ShapeDtypeStruct(u_tm.shape, BF16), cb, False)
    return jnp.swapaxes(y_tm.reshape(seq, b, d), 0, 1)


def _layer(x, mem, p, li, s5_ops, tables, perm):
    b, seq, d = x.shape
    tt = min(DENSE_TOKENS, seq)
    tq = min(ATTN_QUERY_ROWS, seq)
    row = lambda a: a.reshape(1, -1)
    tile_to_d = lambda a: jnp.tile(a, d // a.shape[0]).reshape(1, d)
    head_of_lane = (lax.broadcasted_iota(jnp.int32, (d, LANES), 0) // DIFF_DH
                    == lax.broadcasted_iota(jnp.int32, (d, LANES), 1))
    hsel = jnp.where(head_of_lane, 1.0 / DIFF_DH, 0.0).astype(BF16)
    hexp = head_of_lane.T.astype(BF16)
    cos, sin = tables

    u, q, k, v, qm, gates = proj(x, row(p["norm_mix_g"][li]), p["w_in"][li].astype(BF16), hsel, hexp, row(p["b_gate"][li]),
                                 tile_to_d(p["diff_q_g"][li]), tile_to_d(p["diff_k_g"][li]), tile_to_d(p["mem_q_g"][li]),
                                 cos[:seq], sin[:seq], tt)

    y = _s5_mixer(u, s5_ops, perm)

    lambda_init = 0.8 - 0.6 * math.exp(-0.3 * li)
    attn_args = (q, k, v, row(p["diff_lq1"][li]), row(p["diff_lk1"][li]), row(p["diff_lq2"][li]),
                 row(p["diff_lk2"][li]), row(p["diff_sub_g"][li]))
    score_bound = (1.05 * DIFF_DH ** 0.5 * LOG2E) * jnp.max(jnp.abs(p["diff_q_g"][li])) * jnp.max(jnp.abs(p["diff_k_g"][li]))
    do = lax.cond(score_bound <= SCORE_BOUND,
                  lambda *a: diff_attn(*a, lambda_init, tq, True),
                  lambda *a: diff_attn(*a, lambda_init, tq, False), *attn_args)

    mk, mv = mem_kv(mem, row(p["mem_norm_g"][li]), p["w_mem_kv"][li].astype(BF16), tile_to_d(p["mem_k_g"][li]))
    x1 = merge(x, y, do, qm, gates, mk, mv, p["s5_w_glu"][li].astype(BF16), p["w_branch"][li].astype(BF16),
               p["w_out"][li].astype(BF16), tt)
    return ffn(x1, row(p["ffn_norm_g"][li]), p["w_gate_up"][li].astype(BF16), p["w_down"][li].astype(BF16), tt)


def kernel(x_prompt, x_sample, mem_prompt, mem_sample, norm_mix_g, w_in, b_gate, s5_lam_re, s5_lam_im, s5_log_dt, s5_b_re, s5_b_im, s5_c_re, s5_c_im, s5_d, s5_w_glu, diff_q_g, diff_k_g, diff_lq1, diff_lk1, diff_lq2, diff_lk2, diff_sub_g, mem_norm_g, w_mem_kv, mem_q_g, mem_k_g, w_branch, w_out, ffn_norm_g, w_gate_up, w_down):
    p = dict(norm_mix_g=norm_mix_g, w_in=w_in, b_gate=b_gate, s5_w_glu=s5_w_glu, diff_q_g=diff_q_g, diff_k_g=diff_k_g,
             diff_lq1=diff_lq1, diff_lk1=diff_lk1, diff_lq2=diff_lq2, diff_lk2=diff_lk2, diff_sub_g=diff_sub_g,
             mem_norm_g=mem_norm_g, w_mem_kv=w_mem_kv, mem_q_g=mem_q_g, mem_k_g=mem_k_g, w_branch=w_branch,
             w_out=w_out, ffn_norm_g=ffn_norm_g, w_gate_up=w_gate_up, w_down=w_down)
    tables = rope_tables(max(x_prompt.shape[1], x_sample.shape[1]))
    perm = jnp.asarray(_chunk_permutation(), BF16)
    y_prompt, y_sample = x_prompt, x_sample
    for li in range(norm_mix_g.shape[0]):
        s5_ops = s5_prep(s5_lam_re[li], s5_lam_im[li], s5_log_dt[li], s5_b_re[li], s5_b_im[li],
                         s5_c_re[li], s5_c_im[li], s5_d[li])
        y_prompt = _layer(y_prompt, mem_prompt, p, li, s5_ops, tables, perm)
        y_sample = _layer(y_sample, mem_sample, p, li, s5_ops, tables, perm)
    return (y_prompt, y_sample)
```

```python
import functools
import math

import numpy as np
import jax
import jax.numpy as jnp
from jax import lax
from jax.experimental import pallas as pl
from jax.experimental.pallas import tpu as pltpu

F32 = jnp.float32
BF16 = jnp.bfloat16
EPS = 1e-6
ROPE_THETA = 10000.0
LOG2E = math.log2(math.e)

S5_GROUP = 16
S5_STATE = 64
CHUNK = 16
DIFF_HEADS = 8
DIFF_DH = 64
MEM_HEADS = 4
LANES = 128
GROUPS_PER_LANE_BLOCK = LANES // S5_GROUP
VMEM_LIMIT = 56 * 1024 * 1024
DENSE_TOKENS = 512
ATTN_QUERY_ROWS = 256
ATTN_SUBTILES = 8
ATTN_KEY_BLOCK = 512
S5_CHUNKS_PER_STEP = 64
S5_GROUPS_PER_STEP = 2
SCORE_BOUND = 60.0

HIGHEST = lax.Precision.HIGHEST


def _dot(a, b):
    return jnp.dot(a, b, preferred_element_type=F32)


def _dot_nt(a, b):
    return lax.dot_general(a, b, (((1,), (1,)), ((), ())), preferred_element_type=F32)


def _dot_hi(a, b):
    return jnp.dot(a, b, precision=HIGHEST, preferred_element_type=F32)


def _params(*sem):
    return pltpu.CompilerParams(dimension_semantics=sem, vmem_limit_bytes=VMEM_LIMIT)


def _resident(shape):
    nd = len(shape)
    return pl.BlockSpec(shape, lambda *_: (0,) * nd, pipeline_mode=pl.Buffered(1))


def _rope_kernel(cos_ref, sin_ref):
    rows = cos_ref.shape[0]
    base = pl.program_id(0) * rows
    pos = (lax.broadcasted_iota(jnp.int32, (rows, LANES), 0) + base).astype(F32)
    lane = lax.broadcasted_iota(jnp.int32, (rows, LANES), 1)
    half = DIFF_DH // 2
    freq = (lane % half).astype(F32)
    inv = jnp.exp(freq * (-math.log(ROPE_THETA) / half))
    ang = pos * inv
    cos_ref[...] = jnp.cos(ang)
    s = jnp.sin(ang)
    sin_ref[...] = jnp.where((lane % DIFF_DH) < half, -s, s)


def rope_tables(seq):
    rows = min(seq, 512)
    return pl.pallas_call(
        _rope_kernel,
        grid=(seq // rows,),
        out_specs=[pl.BlockSpec((rows, LANES), lambda i: (i, 0))] * 2,
        out_shape=[jax.ShapeDtypeStruct((seq, LANES), F32)] * 2,
        name="rope_tables",
    )()


def _proj_kernel(x_ref, g_ref, w_ref, hsel_ref, hexp_ref, bg_ref, qg_ref, kg_ref, mqg_ref, cos_ref, sin_ref,
                 u_ref, q_ref, k_ref, v_ref, qm_ref, gt_ref):
    rows, d = x_ref.shape[1:]
    half = rows // 2 if rows % 512 == 0 else rows
    for r0 in range(0, rows, half):
        _proj_rows(slice(r0, r0 + half), d, x_ref, g_ref, w_ref, hsel_ref, hexp_ref, bg_ref, qg_ref, kg_ref, mqg_ref,
                   cos_ref, sin_ref, u_ref, q_ref, k_ref, v_ref, qm_ref, gt_ref)


def _proj_rows(rs, d, x_ref, g_ref, w_ref, hsel_ref, hexp_ref, bg_ref, qg_ref, kg_ref, mqg_ref, cos_ref, sin_ref,
               u_ref, q_ref, k_ref, v_ref, qm_ref, gt_ref):
    x = x_ref[0, rs, :]
    hn = (x * lax.rsqrt(jnp.mean(x * x, axis=-1, keepdims=True) + EPS) * g_ref[...]).astype(BF16)

    def mm(j):
        return _dot(hn, w_ref[:, j * d:(j + 1) * d])

    cos = cos_ref[rs, :]
    sin = sin_ref[rs, :]
    lane = lax.broadcasted_iota(jnp.int32, cos.shape, 1)
    first_half = (lane % DIFF_DH) < (DIFF_DH // 2)

    def head_sums(t):
        return _dot((t * t).astype(BF16), hsel_ref[...]).astype(BF16)

    def rope_store(t, sums, gain, scale, out_ref):
        tn = t * lax.rsqrt(_dot(sums, hexp_ref[...]) + EPS) * gain
        for j in range(d // LANES):
            blk = tn[:, j * LANES:(j + 1) * LANES]
            partner = jnp.where(first_half, pltpu.roll(blk, LANES - DIFF_DH // 2, 1), pltpu.roll(blk, DIFF_DH // 2, 1))
            out_ref[0, rs, j * LANES:(j + 1) * LANES] = ((blk * cos + partner * sin) * scale).astype(BF16)

    def gate_store(j, gl):
        gt_ref[0, rs, j * d:(j + 1) * d] = jax.nn.sigmoid(gl + bg_ref[:, j * d:(j + 1) * d]).astype(BF16)

    q = mm(1)
    k = mm(2)
    q_sums = head_sums(q)
    gl0 = mm(5)
    k_sums = head_sums(k)
    gl1 = mm(6)
    rope_store(q, q_sums, qg_ref[...], DIFF_DH ** -0.5 * LOG2E, q_ref)
    gl2 = mm(7)
    rope_store(k, k_sums, kg_ref[...], 1.0, k_ref)
    qm = mm(4)
    gate_store(0, gl0)
    gate_store(1, gl1)
    u = mm(0)
    gate_store(2, gl2)
    mdh = d // MEM_HEADS
    for h in range(MEM_HEADS):
        seg = qm[:, h * mdh:(h + 1) * mdh]
        segn = seg * lax.rsqrt(jnp.mean(seg * seg, axis=-1, keepdims=True) + EPS) * mqg_ref[:, h * mdh:(h + 1) * mdh]
        qm_ref[0, rs, h * mdh:(h + 1) * mdh] = (segn * (mdh ** -0.5)).astype(BF16)
    v = mm(3)
    u_ref[0, rs, :] = u.astype(BF16)
    v_ref[0, rs, :] = v.astype(BF16)


def proj(x, g, w_in, hsel, hexp, b_gate, qg, kg, mqg, cos, sin, tt):
    b, seq, d = x.shape
    ncol = w_in.shape[1]
    tok = lambda width: pl.BlockSpec((1, tt, width), lambda i, j: (i, j, 0))
    outs = [d, d, d, d, d, 3 * d]
    return pl.pallas_call(
        _proj_kernel,
        grid=(b, seq // tt),
        in_specs=[tok(d), _resident((1, d)), _resident((d, ncol)), _resident((d, LANES)), _resident((LANES, d)),
                  _resident((1, 3 * d)),
                  _resident((1, d)), _resident((1, d)), _resident((1, d)),
                  pl.BlockSpec((tt, LANES), lambda i, j: (j, 0)), pl.BlockSpec((tt, LANES), lambda i, j: (j, 0))],
        out_specs=[tok(w) for w in outs],
        out_shape=[jax.ShapeDtypeStruct((b, seq, w), BF16) for w in outs],
        compiler_params=_params("parallel", "parallel"),
        name="proj",
    )(x, g, w_in, hsel, hexp, b_gate, qg, kg, mqg, cos, sin)


def _s5_prep_kernel(lr_ref, li_ref, ldt_ref, btr_ref, bti_ref, lcr_ref, lci_ref, ldtc_ref, ctr_ref, cti_ref, d_ref,
                    w1_ref, w2_ref, a_ref):
    p, h, c = S5_STATE, S5_GROUP, CHUNK
    width = c * h
    lane128 = lax.broadcasted_iota(jnp.int32, (1, LANES), 1)
    fwd128 = lane128 < p

    dt = jnp.exp(ldt_ref[0])
    zr, zi = lr_ref[0] * dt, li_ref[0] * dt
    kk = jnp.minimum(lax.broadcasted_iota(jnp.int32, (24, 1), 0), c).astype(F32)
    mag = jnp.exp(kk * zr)
    pwr, pwi = mag * jnp.cos(kk * zi), mag * jnp.sin(kk * zi)

    lr, li = lr_ref[0], li_ref[0]
    nr, ni = pwr[1:2] - 1.0, pwi[1:2]
    den = lr * lr + li * li
    cr, ci = (nr * lr + ni * li) / den, (ni * lr - nr * li) / den
    btr, bti = btr_ref[0], bti_ref[0]
    bbr, bbi = cr * btr - ci * bti, cr * bti + ci * btr

    for s in range(c):
        sr = jnp.where(fwd128, pwr[c - 1 - s:c - s], pwr[s:s + 1])
        si = jnp.where(fwd128, pwi[c - 1 - s:c - s], pwi[s:s + 1])
        w1_ref[0, s * h:(s + 1) * h, width:width + LANES] = (bbr * sr - bbi * si).astype(BF16)
        w1_ref[0, s * h:(s + 1) * h, width + LANES:width + 2 * LANES] = (bbr * si + bbi * sr).astype(BF16)

    krow = jnp.minimum(lax.broadcasted_iota(jnp.int32, (1, LANES), 1), c).astype(F32)
    sel_k = lax.broadcasted_iota(jnp.int32, (LANES, width), 0)
    sel_t = lax.broadcasted_iota(jnp.int32, (LANES, width), 1) // h

    def expand(pw, power_of_t):
        return _dot_hi(pw, (sel_k == power_of_t(sel_t)).astype(F32))

    zr_stack, zi_stack = [], []
    for dr in range(2):
        dtc = jnp.exp(ldtc_ref[dr, 0])
        zrc, zic = lcr_ref[dr, 0] * dtc, lci_ref[dr, 0] * dtc
        magc = jnp.exp(zrc * krow)
        pcr, pci = magc * jnp.cos(zic * krow), magc * jnp.sin(zic * krow)
        ctr, cti = ctr_ref[dr, 0], cti_ref[dr, 0]
        lag = (lambda t: t) if dr == 0 else (lambda t: c - 1 - t)
        er, ei = expand(pcr, lag), expand(pci, lag)
        zr_stack.append(er * ctr - ei * cti)
        zi_stack.append(er * cti + ei * ctr)
        rd = (lambda t: t + 1) if dr == 0 else (lambda t: c - t)
        er, ei = expand(pcr, rd), expand(pci, rd)
        re_rows = (er * ctr - ei * cti).astype(BF16)
        im_rows = (-(er * cti + ei * ctr)).astype(BF16)
        w2_ref[0, dr * p:(dr + 1) * p] = re_rows
        w2_ref[0, LANES + dr * p:LANES + (dr + 1) * p] = im_rows

    zr_all = jnp.concatenate(zr_stack, axis=0)
    zi_all = jnp.concatenate(zi_stack, axis=0)
    lane16 = lax.broadcasted_iota(jnp.int32, (h, LANES), 1)
    g_f = _dot_hi(jnp.where(lane16 < p, bbr, 0.0), zr_all) - _dot_hi(jnp.where(lane16 < p, bbi, 0.0), zi_all)
    g_b = _dot_hi(jnp.where(lane16 < p, 0.0, bbr), zr_all) - _dot_hi(jnp.where(lane16 < p, 0.0, bbi), zi_all)

    lane = lax.broadcasted_iota(jnp.int32, (h, width), 1)
    row = lax.broadcasted_iota(jnp.int32, (h, width), 0)
    dcol = d_ref[0]
    for s in range(c):
        m_f = jnp.where(lane >= h * s, g_f if s == 0 else pltpu.roll(g_f, h * s, 1), 0.0)
        m_b = jnp.where(lane < h * (s + 1), g_b if s == c - 1 else pltpu.roll(g_b, h * (s + 1), 1), 0.0)
        skip = jnp.where(lane == h * s + row, dcol, 0.0)
        w1_ref[0, s * h:(s + 1) * h, 0:width] = (m_f + m_b + skip).astype(BF16)

    r8 = lax.broadcasted_iota(jnp.int32, (8, LANES), 0)
    a_ref[0] = jnp.where(r8 == 0, pwr[c:c + 1], jnp.where(r8 == 1, pwi[c:c + 1], 0.0))


def s5_prep(lam_re, lam_im, log_dt, b_re, b_im, c_re, c_im, d):
    _, g, p = lam_re.shape
    h = S5_GROUP
    width = CHUNK * h
    pack = lambda a: jnp.concatenate([a[0], a[1]], axis=-1)
    lr = pack(lam_re)[:, None, :]
    li = pack(lam_im)[:, None, :]
    ldt = pack(jnp.broadcast_to(log_dt[..., None], (2, g, p)))[:, None, :]
    btr = pack(jnp.swapaxes(b_re, -1, -2))
    bti = pack(jnp.swapaxes(b_im, -1, -2))
    ctr = jnp.tile(jnp.swapaxes(c_re, -1, -2), (1, 1, 1, CHUNK))
    cti = jnp.tile(jnp.swapaxes(c_im, -1, -2), (1, 1, 1, CHUNK))
    row = lambda last: pl.BlockSpec((1, 1, last), lambda i: (i, 0, 0))
    col = lambda rows, last: pl.BlockSpec((2, 1, rows, last), lambda i: (0, i, 0, 0))
    return pl.pallas_call(
        _s5_prep_kernel,
        grid=(g,),
        in_specs=[row(LANES), row(LANES), row(LANES),
                  pl.BlockSpec((1, h, LANES), lambda i: (i, 0, 0)), pl.BlockSpec((1, h, LANES), lambda i: (i, 0, 0)),
                  col(p, 1), col(p, 1), col(1, 1), col(p, width), col(p, width),
                  pl.BlockSpec((1, h, 1), lambda i: (i, 0, 0))],
        out_specs=[pl.BlockSpec((1, width, 2 * width), lambda i: (i, 0, 0)),
                   pl.BlockSpec((1, width, width), lambda i: (i, 0, 0)),
                   pl.BlockSpec((1, 8, LANES), lambda i: (i, 0, 0))],
        out_shape=[jax.ShapeDtypeStruct((g, width, 2 * width), BF16),
                   jax.ShapeDtypeStruct((g, width, width), BF16),
                   jax.ShapeDtypeStruct((g, 8, LANES), F32)],
        compiler_params=_params("parallel"),
        name="s5_prep",
    )(lr, li, ldt, btr, bti, lam_re[..., None], lam_im[..., None], log_dt[..., None, None], ctr, cti,
      d.reshape(g, h, 1))


def _chunk_permutation():
    n, half = GROUPS_PER_LANE_BLOCK, CHUNK // 2
    pm = np.zeros((half * LANES, n * half * S5_GROUP), np.float32)
    for t in range(half):
        for g in range(n):
            for h in range(S5_GROUP):
                pm[t * LANES + g * S5_GROUP + h, g * LANES + t * S5_GROUP + h] = 1.0
    return pm


def _to_chunks_kernel(x_ref, p_ref, o_ref):
    cb, _, bsz, _ = x_ref.shape
    rows, half = cb * bsz, CHUNK // 2
    for hf in range(2):
        x8 = jnp.concatenate([x_ref[:, hf * half + t].reshape(rows, LANES) for t in range(half)], axis=1)
        out = _dot(x8, p_ref[...])
        for g in range(GROUPS_PER_LANE_BLOCK):
            o_ref[g, :, hf * LANES:(hf + 1) * LANES] = out[:, g * LANES:(g + 1) * LANES].astype(BF16)


def _from_chunks_kernel(y_ref, p_ref, o_ref):
    cb, _, bsz, _ = o_ref.shape
    half = CHUNK // 2
    for hf in range(2):
        y8 = jnp.concatenate([y_ref[g, :, hf * LANES:(hf + 1) * LANES] for g in range(GROUPS_PER_LANE_BLOCK)], axis=1)
        out = _dot_nt(y8, p_ref[...])
        for t in range(half):
            o_ref[:, hf * half + t] = out[:, t * LANES:(t + 1) * LANES].astype(BF16).reshape(cb, bsz, LANES)


def _chunk_relayout(kern, x, perm, out_struct, cb, to_chunks):
    n = GROUPS_PER_LANE_BLOCK
    if to_chunks:
        nc, _, bsz, d = x.shape
    else:
        nc, _, bsz, d = out_struct.shape
    tm_spec = pl.BlockSpec((cb, CHUNK, bsz, LANES), lambda i, j: (i, 0, 0, j))
    ch_spec = pl.BlockSpec((n, cb * bsz, CHUNK * S5_GROUP), lambda i, j: (j, i, 0))
    return pl.pallas_call(
        kern,
        grid=(nc // cb, d // LANES),
        in_specs=[tm_spec if to_chunks else ch_spec, _resident(perm.shape)],
        out_specs=ch_spec if to_chunks else tm_spec,
        out_shape=out_struct,
        compiler_params=_params("parallel", "parallel"),
        name="to_chunks" if to_chunks else "from_chunks",
    )(x, perm)


def _s5_chunk_kernel(u_ref, w1_ref, w2_ref, a_ref, y_ref, yint_ref, sin_ref, st_ref, *, rows_per_chunk):
    gb, nrows, width = u_ref.shape
    rb = rows_per_chunk
    n_chunks = nrows // rb
    ns = S5_STATE
    fwd = lax.broadcasted_iota(jnp.int32, (rb, LANES), 1) < ns

    for g in range(gb):
        r = _dot(u_ref[g], w1_ref[g])
        yint_ref[...] = r[:, :width]
        sin_ref[...] = r[:, width:]
        a_re = a_ref[g, 0:1, :]
        a_im = a_ref[g, 1:2, :]

        def step(i, carry):
            s_re, s_im = carry
            rf = pl.ds(pl.multiple_of(i * rb, rb), rb)
            rr = pl.ds(pl.multiple_of((n_chunks - 1 - i) * rb, rb), rb)
            st_ref[rf, 0:ns] = s_re[:, 0:ns]
            st_ref[rf, LANES:LANES + ns] = s_im[:, 0:ns]
            st_ref[rr, ns:LANES] = s_re[:, ns:LANES]
            st_ref[rr, LANES + ns:2 * LANES] = s_im[:, ns:LANES]
            in_re = jnp.where(fwd, sin_ref[rf, 0:LANES], sin_ref[rr, 0:LANES])
            in_im = jnp.where(fwd, sin_ref[rf, LANES:2 * LANES], sin_ref[rr, LANES:2 * LANES])
            return (a_re * s_re - a_im * s_im + in_re, a_re * s_im + a_im * s_re + in_im)

        zero = jnp.zeros((rb, LANES), F32)
        lax.fori_loop(0, n_chunks, step, (zero, zero), unroll=4)
        y_ref[g] = (yint_ref[...] + _dot(st_ref[...].astype(BF16), w2_ref[g])).astype(BF16)


def s5_chunk(u2, w1, w2, a, rows_per_chunk, gb):
    g, nrows, width = u2.shape
    kern = functools.partial(_s5_chunk_kernel, rows_per_chunk=rows_per_chunk)
    return pl.pallas_call(
        kern,
        grid=(g // gb,),
        in_specs=[pl.BlockSpec((gb, nrows, width), lambda i: (i, 0, 0)),
                  pl.BlockSpec((gb, width, 2 * width), lambda i: (i, 0, 0)),
                  pl.BlockSpec((gb, width, width), lambda i: (i, 0, 0)),
                  pl.BlockSpec((gb, 8, LANES), lambda i: (i, 0, 0))],
        out_specs=pl.BlockSpec((gb, nrows, width), lambda i: (i, 0, 0)),
        out_shape=jax.ShapeDtypeStruct((g, nrows, width), BF16),
        scratch_shapes=[pltpu.VMEM((nrows, width), F32), pltpu.VMEM((nrows, width), F32),
                        pltpu.VMEM((nrows, width), F32)],
        compiler_params=_params("parallel"),
        name="s5_chunk",
    )(u2, w1, w2, a)


def _diff_lambda(lq1_ref, lk1_ref, lq2_ref, lk2_ref, lambda_init):
    return (jnp.exp(jnp.sum(lq1_ref[...] * lk1_ref[...], axis=-1, keepdims=True))
            - jnp.exp(jnp.sum(lq2_ref[...] * lk2_ref[...], axis=-1, keepdims=True)) + lambda_init)


def _map_queries(q):
    lane = lax.broadcasted_iota(jnp.int32, q.shape, 1)
    zero = jnp.zeros_like(q)
    return jnp.where(lane < DIFF_DH, q, zero), jnp.where(lane < DIFF_DH, zero, q)


def _sublayer_norm_store(o, sg_ref, o_ref, lambda_init):
    on = o * lax.rsqrt(jnp.mean(o * o, axis=-1, keepdims=True) + EPS) * sg_ref[...]
    o_ref[0] = (on * (1.0 - lambda_init)).astype(BF16)


def _diff_attn_stream_kernel(q_ref, k_ref, v_ref, lq1_ref, lk1_ref, lq2_ref, lk2_ref, sg_ref, o_ref, e_ref, *,
                             lambda_init, tk, sub):
    tq = q_ref.shape[1]
    seq = k_ref.shape[1]
    lam = _diff_lambda(lq1_ref, lk1_ref, lq2_ref, lk2_ref, lambda_init)
    n_sub = tq // sub

    nkb = seq // tk

    def score_block(qmaps, h, kb, part):
        for mp in (0, 1):
            e = jnp.exp2(_dot_nt(qmaps[mp], k_ref[0, kb * tk:(kb + 1) * tk, :]))
            for c in range(tk // LANES):
                part[mp] = part[mp] + e[:, c * LANES:(c + 1) * LANES]
            e_ref[h % 2, mp, :, kb * tk:(kb + 1) * tk] = e.astype(BF16)

    def value_block(h, kb, mu, acc):
        cols = []
        for c in range(tk // LANES):
            sl = slice(kb * tk + c * LANES, kb * tk + (c + 1) * LANES)
            cols.append(e_ref[h % 2, 0, :, sl] - mu * e_ref[h % 2, 1, :, sl])
        term = _dot(jnp.concatenate(cols, axis=1), v_ref[0, kb * tk:(kb + 1) * tk, :])
        return term if acc is None else acc + term

    prev = None
    for h in range(n_sub + 1):
        if h < n_sub:
            qmaps = _map_queries(q_ref[0, h * sub:(h + 1) * sub, :])
            part = [jnp.zeros((sub, LANES), F32), jnp.zeros((sub, LANES), F32)]
        acc = None
        for kb in range(nkb):
            if h < n_sub:
                score_block(qmaps, h, kb, part)
            if prev is not None:
                acc = value_block(h - 1, kb, prev[1], acc)
        if prev is not None:
            o = acc * (1.0 / prev[0])
            on = o * lax.rsqrt(jnp.mean(o * o, axis=-1, keepdims=True) + EPS) * sg_ref[...]
            o_ref[0, (h - 1) * sub:h * sub, :] = (on * (1.0 - lambda_init)).astype(BF16)
        if h < n_sub:
            l1, l2 = [jnp.sum(pt, axis=-1, keepdims=True) for pt in part]
            prev = (l1, jnp.broadcast_to(lam * l1 / l2, (sub, LANES)).astype(BF16))


def _diff_attn_exact_kernel(q_ref, k_ref, v_ref, lq1_ref, lk1_ref, lq2_ref, lk2_ref, sg_ref, o_ref, *, lambda_init):
    lam = _diff_lambda(lq1_ref, lk1_ref, lq2_ref, lk2_ref, lambda_init)
    k = k_ref[0]

    def softmax_parts(s):
        e = jnp.exp2(s - jnp.max(s, axis=-1, keepdims=True))
        return e, 1.0 / jnp.sum(e, axis=-1, keepdims=True)

    q1, q2 = _map_queries(q_ref[0])
    e1, r1 = softmax_parts(_dot_nt(q1, k))
    e2, r2 = softmax_parts(_dot_nt(q2, k))
    a = e1 * r1 - e2 * (r2 * lam)
    _sublayer_norm_store(_dot(a.astype(BF16), v_ref[0]), sg_ref, o_ref, lambda_init)


def diff_attn(q, k, v, lq1, lk1, lq2, lk2, sub_g, lambda_init, tq, streaming):
    b, seq, d = q.shape
    hw = 2 * DIFF_DH
    small = lambda n: pl.BlockSpec((1, n), lambda i, h, j: (0, 0))
    scratch = []
    if streaming:
        sub = tq
        tq = min(ATTN_SUBTILES * sub, seq)
        kern = functools.partial(_diff_attn_stream_kernel, lambda_init=lambda_init, tk=min(ATTN_KEY_BLOCK, seq), sub=sub)
        scratch = [pltpu.VMEM((2, 2, sub, seq), BF16)]
    else:
        kern = functools.partial(_diff_attn_exact_kernel, lambda_init=lambda_init)
    return pl.pallas_call(
        kern,
        grid=(b, DIFF_HEADS, seq // tq),
        in_specs=[pl.BlockSpec((1, tq, hw), lambda i, h, j: (i, j, h)),
                  pl.BlockSpec((1, seq, hw), lambda i, h, j: (i, 0, h)),
                  pl.BlockSpec((1, seq, hw), lambda i, h, j: (i, 0, h)),
                  small(DIFF_DH), small(DIFF_DH), small(DIFF_DH), small(DIFF_DH), small(hw)],
        out_specs=pl.BlockSpec((1, tq, hw), lambda i, h, j: (i, j, h)),
        out_shape=jax.ShapeDtypeStruct((b, seq, d), BF16),
        scratch_shapes=scratch,
        compiler_params=_params("parallel", "parallel", "parallel"),
        name="diff_attn_stream" if streaming else "diff_attn_exact",
    )(q, k, v, lq1, lk1, lq2, lk2, sub_g)


def _mem_kv_kernel(m_ref, g_ref, w_ref, kg_ref, k_ref, v_ref):
    d = m_ref.shape[-1]
    m = m_ref[0]
    mn = (m * lax.rsqrt(jnp.mean(m * m, axis=-1, keepdims=True) + EPS) * g_ref[...]).astype(BF16)
    kv = _dot(mn, w_ref[...])
    mdh = d // MEM_HEADS
    for h in range(MEM_HEADS):
        seg = kv[:, h * mdh:(h + 1) * mdh]
        segn = seg * lax.rsqrt(jnp.mean(seg * seg, axis=-1, keepdims=True) + EPS) * kg_ref[:, h * mdh:(h + 1) * mdh]
        k_ref[0, :, h * mdh:(h + 1) * mdh] = segn.astype(BF16)
    v_ref[0] = kv[:, d:].astype(BF16)


def mem_kv(mem, g, w_kv, kg):
    b, m, d = mem.shape
    blk = pl.BlockSpec((1, m, d), lambda i: (i, 0, 0))
    return pl.pallas_call(
        _mem_kv_kernel,
        grid=(b,),
        in_specs=[blk, _resident((1, d)), _resident((d, 2 * d)), _resident((1, d))],
        out_specs=[blk, blk],
        out_shape=[jax.ShapeDtypeStruct((b, m, d), BF16)] * 2,
        compiler_params=_params("parallel"),
        name="mem_kv",
    )(mem, g, w_kv, kg)


def _merge_kernel(x_ref, y_ref, do_ref, qm_ref, gt_ref, mk_ref, mv_ref, wglu_ref, wb_ref, wo_ref, o_ref):
    d = x_ref.shape[-1]
    mdh = d // MEM_HEADS
    heads = [slice(h * mdh, (h + 1) * mdh) for h in range(MEM_HEADS)]

    def gated(n, br):
        return gt_ref[0, :, n * d:(n + 1) * d].astype(F32) * _dot(br, wb_ref[n])

    merged = gated(1, do_ref[0])
    qm = qm_ref[0]
    mk = mk_ref[0]
    scores = [_dot_nt(qm[:, sl], mk[:, sl]) for sl in heads]

    z = jax.nn.gelu(y_ref[0].astype(F32))
    s5o = (z * jax.nn.sigmoid(_dot(z.astype(BF16), wglu_ref[...]))).astype(BF16)

    mv = mv_ref[0]
    mem_parts = []
    for s, sl in zip(scores, heads):
        e = jnp.exp(s - jnp.max(s, axis=-1, keepdims=True))
        pr = e * (1.0 / jnp.sum(e, axis=-1, keepdims=True))
        mem_parts.append(_dot(pr.astype(BF16), mv[:, sl]).astype(BF16))
    mo = jnp.concatenate(mem_parts, axis=-1)

    merged = merged + gated(0, s5o) + gated(2, mo)
    o_ref[0] = x_ref[0] + _dot(merged.astype(BF16), wo_ref[...])


def merge(x, y, do, qm, gates, mk, mv, w_glu, w_branch, w_out, tt):
    b, seq, d = x.shape
    m = mk.shape[1]
    tok = lambda width: pl.BlockSpec((1, tt, width), lambda i, j: (i, j, 0))
    memb = pl.BlockSpec((1, m, d), lambda i, j: (i, 0, 0))
    return pl.pallas_call(
        _merge_kernel,
        grid=(b, seq // tt),
        in_specs=[tok(d), tok(d), tok(d), tok(d), tok(3 * d), memb, memb,
                  _resident((d, d)), _resident((3, d, d)), _resident((d, d))],
        out_specs=tok(d),
        out_shape=jax.ShapeDtypeStruct((b, seq, d), F32),
        compiler_params=_params("parallel", "parallel"),
        name="merge",
    )(x, y, do, qm, gates, mk, mv, w_glu, w_branch, w_out)


def _ffn_kernel(x_ref, g_ref, wgu_ref, wd_ref, o_ref):
    dff = wd_ref.shape[0]
    rows = x_ref.shape[1]
    half = rows // 2 if rows % 512 == 0 else rows
    for r0 in range(0, rows, half):
        x = x_ref[0, r0:r0 + half, :]
        hn = (x * lax.rsqrt(jnp.mean(x * x, axis=-1, keepdims=True) + EPS) * g_ref[...]).astype(BF16)
        gate = _dot(hn, wgu_ref[:, :dff])
        up = _dot(hn, wgu_ref[:, dff:])
        act = (jax.nn.silu(gate) * up).astype(BF16)
        o_ref[0, r0:r0 + half, :] = x + _dot(act, wd_ref[...])


def ffn(x, g, w_gate_up, w_down, tt):
    b, seq, d = x.shape
    dff = w_down.shape[0]
    tok = pl.BlockSpec((1, tt, d), lambda i, j: (i, j, 0))
    return pl.pallas_call(
        _ffn_kernel,
        grid=(b, seq // tt),
        in_specs=[tok, _resident((1, d)), _resident((d, 2 * dff)), _resident((dff, d))],
        out_specs=tok,
        out_shape=jax.ShapeDtypeStruct((b, seq, d), F32),
        compiler_params=_params("parallel", "parallel"),
        name="ffn",
    )(x, g, w_gate_up, w_down)


def _s5_mixer(u, s5_ops, perm):
    b, seq, d = u.shape
    nc = seq // CHUNK
    cb = min(S5_CHUNKS_PER_STEP, nc)
    w1, w2, a = s5_ops
    u_tm = jnp.swapaxes(u, 0, 1).reshape(nc, CHUNK, b, d)
    chunk_rows = jax.ShapeDtypeStruct((d // S5_GROUP, nc * b, CHUNK * S5_GROUP), BF16)
    u2 = _chunk_relayout(_to_chunks_kernel, u_tm, perm, chunk_rows, cb, True)
    y2 = s5_chunk(u2, w1, w2, a, b, S5_GROUPS_PER_STEP)
    y_tm = _chunk_relayout(_from_chunks_kernel, y2, perm, jax.ShapeDtypeStruct(u_tm.shape, BF16), cb, False)
    return jnp.swapaxes(y_tm.reshape(seq, b, d), 0, 1)


def _layer(x, mem, p, li, s5_ops, tables, perm):
    b, seq, d = x.shape
    tt = min(DENSE_TOKENS, seq)
    tq = min(ATTN_QUERY_ROWS, seq)
    row = lambda a: a.reshape(1, -1)
    tile_to_d = lambda a: jnp.tile(a, d // a.shape[0]).reshape(1, d)
    head_of_lane = (lax.broadcasted_iota(jnp.int32, (d, LANES), 0) // DIFF_DH
                    == lax.broadcasted_iota(jnp.int32, (d, LANES), 1))
    hsel = jnp.where(head_of_lane, 1.0 / DIFF_DH, 0.0).astype(BF16)
    hexp = head_of_lane.T.astype(BF16)
    cos, sin = tables

    u, q, k, v, qm, gates = proj(x, row(p["norm_mix_g"][li]), p["w_in"][li].astype(BF16), hsel, hexp, row(p["b_gate"][li]),
                                 tile_to_d(p["diff_q_g"][li]), tile_to_d(p["diff_k_g"][li]), tile_to_d(p["mem_q_g"][li]),
                                 cos[:seq], sin[:seq], tt)

    y = _s5_mixer(u, s5_ops, perm)

    lambda_init = 0.8 - 0.6 * math.exp(-0.3 * li)
    attn_args = (q, k, v, row(p["diff_lq1"][li]), row(p["diff_lk1"][li]), row(p["diff_lq2"][li]),
                 row(p["diff_lk2"][li]), row(p["diff_sub_g"][li]))
    score_bound = (1.05 * DIFF_DH ** 0.5 * LOG2E) * jnp.max(jnp.abs(p["diff_q_g"][li])) * jnp.max(jnp.abs(p["diff_k_g"][li]))
    do = lax.cond(score_bound <= SCORE_BOUND,
                  lambda *a: diff_attn(*a, lambda_init, tq, True),
                  lambda *a: diff_attn(*a, lambda_init, tq, False), *attn_args)

    mk, mv = mem_kv(mem, row(p["mem_norm_g"][li]), p["w_mem_kv"][li].astype(BF16), tile_to_d(p["mem_k_g"][li]))
    x1 = merge(x, y, do, qm, gates, mk, mv, p["s5_w_glu"][li].astype(BF16), p["w_branch"][li].astype(BF16),
               p["w_out"][li].astype(BF16), tt)
    return ffn(x1, row(p["ffn_norm_g"][li]), p["w_gate_up"][li].astype(BF16), p["w_down"][li].astype(BF16), tt)


def kernel(x_prompt, x_sample, mem_prompt, mem_sample, norm_mix_g, w_in, b_gate, s5_lam_re, s5_lam_im, s5_log_dt, s5_b_re, s5_b_im, s5_c_re, s5_c_im, s5_d, s5_w_glu, diff_q_g, diff_k_g, diff_lq1, diff_lk1, diff_lq2, diff_lk2, diff_sub_g, mem_norm_g, w_mem_kv, mem_q_g, mem_k_g, w_branch, w_out, ffn_norm_g, w_gate_up, w_down):
    p = dict(norm_mix_g=norm_mix_g, w_in=w_in, b_gate=b_gate, s5_w_glu=s5_w_glu, diff_q_g=diff_q_g, diff_k_g=diff_k_g,
             diff_lq1=diff_lq1, diff_lk1=diff_lk1, diff_lq2=diff_lq2, diff_lk2=diff_lk2, diff_sub_g=diff_sub_g,
             mem_norm_g=mem_norm_g, w_mem_kv=w_mem_kv, mem_q_g=mem_q_g, mem_k_g=mem_k_g, w_branch=w_branch,
             w_out=w_out, ffn_norm_g=ffn_norm_g, w_gate_up=w_gate_up, w_down=w_down)
    tables = rope_tables(max(x_prompt.shape[1], x_sample.shape[1]))
    perm = jnp.asarray(_chunk_permutation(), BF16)
    y_prompt, y_sample = x_prompt, x_sample
    for li in range(norm_mix_g.shape[0]):
        s5_ops = s5_prep(s5_lam_re[li], s5_lam_im[li], s5_log_dt[li], s5_b_re[li], s5_b_im[li],
                         s5_c_re[li], s5_c_im[li], s5_d[li])
        y_prompt = _layer(y_prompt, mem_prompt, p, li, s5_ops, tables, perm)
        y_sample = _layer(y_sample, mem_sample, p, li, s5_ops, tables, perm)
    return (y_prompt, y_sample)
```

```python
import functools
import math

import numpy as np
import jax
import jax.numpy as jnp
from jax import lax
from jax.experimental import pallas as pl
from jax.experimental.pallas import tpu as pltpu

F32 = jnp.float32
BF16 = jnp.bfloat16
EPS = 1e-6
ROPE_THETA = 10000.0
LOG2E = math.log2(math.e)

S5_GROUP = 16
S5_STATE = 64
CHUNK = 16
DIFF_HEADS = 8
DIFF_DH = 64
MEM_HEADS = 4
LANES = 128
GROUPS_PER_LANE_BLOCK = LANES // S5_GROUP
VMEM_LIMIT = 56 * 1024 * 1024
DENSE_TOKENS = 512
ATTN_QUERY_ROWS = 256
ATTN_SUBTILES = 8
ATTN_KEY_BLOCK = 512
S5_CHUNKS_PER_STEP = 64
S5_GROUPS_PER_STEP = 2
SCORE_BOUND = 60.0

HIGHEST = lax.Precision.HIGHEST


def _dot(a, b):
    return jnp.dot(a, b, preferred_element_type=F32)


def _dot_nt(a, b):
    return lax.dot_general(a, b, (((1,), (1,)), ((), ())), preferred_element_type=F32)


def _dot_hi(a, b):
    return jnp.dot(a, b, precision=HIGHEST, preferred_element_type=F32)


def _params(*sem):
    return pltpu.CompilerParams(dimension_semantics=sem, vmem_limit_bytes=VMEM_LIMIT)


def _resident(shape):
    nd = len(shape)
    return pl.BlockSpec(shape, lambda *_: (0,) * nd, pipeline_mode=pl.Buffered(1))


def _rope_kernel(cos_ref, sin_ref):
    rows = cos_ref.shape[0]
    base = pl.program_id(0) * rows
    pos = (lax.broadcasted_iota(jnp.int32, (rows, LANES), 0) + base).astype(F32)
    lane = lax.broadcasted_iota(jnp.int32, (rows, LANES), 1)
    half = DIFF_DH // 2
    freq = (lane % half).astype(F32)
    inv = jnp.exp(freq * (-math.log(ROPE_THETA) / half))
    ang = pos * inv
    cos_ref[...] = jnp.cos(ang)
    s = jnp.sin(ang)
    sin_ref[...] = jnp.where((lane % DIFF_DH) < half, -s, s)


def rope_tables(seq):
    rows = min(seq, 512)
    return pl.pallas_call(
        _rope_kernel,
        grid=(seq // rows,),
        out_specs=[pl.BlockSpec((rows, LANES), lambda i: (i, 0))] * 2,
        out_shape=[jax.ShapeDtypeStruct((seq, LANES), F32)] * 2,
        name="rope_tables",
    )()


def _proj_kernel(x_ref, g_ref, w_ref, hsel_ref, hexp_ref, bg_ref, qg_ref, kg_ref, mqg_ref, cos_ref, sin_ref,
                 u_ref, q_ref, k_ref, v_ref, qm_ref, gt_ref):
    rows, d = x_ref.shape[1:]
    half = rows // 2 if rows % 512 == 0 else rows
    for r0 in range(0, rows, half):
        _proj_rows(slice(r0, r0 + half), d, x_ref, g_ref, w_ref, hsel_ref, hexp_ref, bg_ref, qg_ref, kg_ref, mqg_ref,
                   cos_ref, sin_ref, u_ref, q_ref, k_ref, v_ref, qm_ref, gt_ref)


def _proj_rows(rs, d, x_ref, g_ref, w_ref, hsel_ref, hexp_ref, bg_ref, qg_ref, kg_ref, mqg_ref, cos_ref, sin_ref,
               u_ref, q_ref, k_ref, v_ref, qm_ref, gt_ref):
    x = x_ref[0, rs, :]
    hn = (x * lax.rsqrt(jnp.mean(x * x, axis=-1, keepdims=True) + EPS) * g_ref[...]).astype(BF16)

    def mm(j):
        return _dot(hn, w_ref[:, j * d:(j + 1) * d])

    cos = cos_ref[rs, :]
    sin = sin_ref[rs, :]
    lane = lax.broadcasted_iota(jnp.int32, cos.shape, 1)
    first_half = (lane % DIFF_DH) < (DIFF_DH // 2)

    def head_sums(t):
        return _dot((t * t).astype(BF16), hsel_ref[...]).astype(BF16)

    def rope_store(t, sums, gain, scale, out_ref):
        tn = t * lax.rsqrt(_dot(sums, hexp_ref[...]) + EPS) * gain
        for j in range(d // LANES):
            blk = tn[:, j * LANES:(j + 1) * LANES]
            partner = jnp.where(first_half, pltpu.roll(blk, LANES - DIFF_DH // 2, 1), pltpu.roll(blk, DIFF_DH // 2, 1))
            out_ref[0, rs, j * LANES:(j + 1) * LANES] = ((blk * cos + partner * sin) * scale).astype(BF16)

    def gate_store(j, gl):
        gt_ref[0, rs, j * d:(j + 1) * d] = jax.nn.sigmoid(gl + bg_ref[:, j * d:(j + 1) * d]).astype(BF16)

    q = mm(1)
    k = mm(2)
    q_sums = head_sums(q)
    gl0 = mm(5)
    k_sums = head_sums(k)
    gl1 = mm(6)
    rope_store(q, q_sums, qg_ref[...], DIFF_DH ** -0.5 * LOG2E, q_ref)
    gl2 = mm(7)
    rope_store(k, k_sums, kg_ref[...], 1.0, k_ref)
    qm = mm(4)
    gate_store(0, gl0)
    gate_store(1, gl1)
    u = mm(0)
    gate_store(2, gl2)
    mdh = d // MEM_HEADS
    for h in range(MEM_HEADS):
        seg = qm[:, h * mdh:(h + 1) * mdh]
        segn = seg * lax.rsqrt(jnp.mean(seg * seg, axis=-1, keepdims=True) + EPS) * mqg_ref[:, h * mdh:(h + 1) * mdh]
        qm_ref[0, rs, h * mdh:(h + 1) * mdh] = (segn * (mdh ** -0.5)).astype(BF16)
    v = mm(3)
    u_ref[0, rs, :] = u.astype(BF16)
    v_ref[0, rs, :] = v.astype(BF16)


def proj(x, g, w_in, hsel, hexp, b_gate, qg, kg, mqg, cos, sin, tt):
    b, seq, d = x.shape
    ncol = w_in.shape[1]
    tok = lambda width: pl.BlockSpec((1, tt, width), lambda i, j: (i, j, 0))
    outs = [d, d, d, d, d, 3 * d]
    return pl.pallas_call(
        _proj_kernel,
        grid=(b, seq // tt),
        in_specs=[tok(d), _resident((1, d)), _resident((d, ncol)), _resident((d, LANES)), _resident((LANES, d)),
                  _resident((1, 3 * d)),
                  _resident((1, d)), _resident((1, d)), _resident((1, d)),
                  pl.BlockSpec((tt, LANES), lambda i, j: (j, 0)), pl.BlockSpec((tt, LANES), lambda i, j: (j, 0))],
        out_specs=[tok(w) for w in outs],
        out_shape=[jax.ShapeDtypeStruct((b, seq, w), BF16) for w in outs],
        compiler_params=_params("parallel", "parallel"),
        name="proj",
    )(x, g, w_in, hsel, hexp, b_gate, qg, kg, mqg, cos, sin)


def _s5_prep_kernel(lr_ref, li_ref, ldt_ref, btr_ref, bti_ref, lcr_ref, lci_ref, ldtc_ref, ctr_ref, cti_ref, d_ref,
                    w1_ref, w2_ref, a_ref):
    p, h, c = S5_STATE, S5_GROUP, CHUNK
    width = c * h
    lane128 = lax.broadcasted_iota(jnp.int32, (1, LANES), 1)
    fwd128 = lane128 < p

    dt = jnp.exp(ldt_ref[0])
    zr, zi = lr_ref[0] * dt, li_ref[0] * dt
    kk = jnp.minimum(lax.broadcasted_iota(jnp.int32, (24, 1), 0), c).astype(F32)
    mag = jnp.exp(kk * zr)
    pwr, pwi = mag * jnp.cos(kk * zi), mag * jnp.sin(kk * zi)

    lr, li = lr_ref[0], li_ref[0]
    nr, ni = pwr[1:2] - 1.0, pwi[1:2]
    den = lr * lr + li * li
    cr, ci = (nr * lr + ni * li) / den, (ni * lr - nr * li) / den
    btr, bti = btr_ref[0], bti_ref[0]
    bbr, bbi = cr * btr - ci * bti, cr * bti + ci * btr

    for s in range(c):
        sr = jnp.where(fwd128, pwr[c - 1 - s:c - s], pwr[s:s + 1])
        si = jnp.where(fwd128, pwi[c - 1 - s:c - s], pwi[s:s + 1])
        w1_ref[0, s * h:(s + 1) * h, width:width + LANES] = (bbr * sr - bbi * si).astype(BF16)
        w1_ref[0, s * h:(s + 1) * h, width + LANES:width + 2 * LANES] = (bbr * si + bbi * sr).astype(BF16)

    krow = jnp.minimum(lax.broadcasted_iota(jnp.int32, (1, LANES), 1), c).astype(F32)
    sel_k = lax.broadcasted_iota(jnp.int32, (LANES, width), 0)
    sel_t = lax.broadcasted_iota(jnp.int32, (LANES, width), 1) // h

    def expand(pw, power_of_t):
        return _dot_hi(pw, (sel_k == power_of_t(sel_t)).astype(F32))

    zr_stack, zi_stack = [], []
    for dr in range(2):
        dtc = jnp.exp(ldtc_ref[dr, 0])
        zrc, zic = lcr_ref[dr, 0] * dtc, lci_ref[dr, 0] * dtc
        magc = jnp.exp(zrc * krow)
        pcr, pci = magc * jnp.cos(zic * krow), magc * jnp.sin(zic * krow)
        ctr, cti = ctr_ref[dr, 0], cti_ref[dr, 0]
        lag = (lambda t: t) if dr == 0 else (lambda t: c - 1 - t)
        er, ei = expand(pcr, lag), expand(pci, lag)
        zr_stack.append(er * ctr - ei * cti)
        zi_stack.append(er * cti + ei * ctr)
        rd = (lambda t: t + 1) if dr == 0 else (lambda t: c - t)
        er, ei = expand(pcr, rd), expand(pci, rd)
        re_rows = (er * ctr - ei * cti).astype(BF16)
        im_rows = (-(er * cti + ei * ctr)).astype(BF16)
        w2_ref[0, dr * p:(dr + 1) * p] = re_rows
        w2_ref[0, LANES + dr * p:LANES + (dr + 1) * p] = im_rows

    zr_all = jnp.concatenate(zr_stack, axis=0)
    zi_all = jnp.concatenate(zi_stack, axis=0)
    lane16 = lax.broadcasted_iota(jnp.int32, (h, LANES), 1)
    g_f = _dot_hi(jnp.where(lane16 < p, bbr, 0.0), zr_all) - _dot_hi(jnp.where(lane16 < p, bbi, 0.0), zi_all)
    g_b = _dot_hi(jnp.where(lane16 < p, 0.0, bbr), zr_all) - _dot_hi(jnp.where(lane16 < p, 0.0, bbi), zi_all)

    lane = lax.broadcasted_iota(jnp.int32, (h, width), 1)
    row = lax.broadcasted_iota(jnp.int32, (h, width), 0)
    dcol = d_ref[0]
    for s in range(c):
        m_f = jnp.where(lane >= h * s, g_f if s == 0 else pltpu.roll(g_f, h * s, 1), 0.0)
        m_b = jnp.where(lane < h * (s + 1), g_b if s == c - 1 else pltpu.roll(g_b, h * (s + 1), 1), 0.0)
        skip = jnp.where(lane == h * s + row, dcol, 0.0)
        w1_ref[0, s * h:(s + 1) * h, 0:width] = (m_f + m_b + skip).astype(BF16)

    r8 = lax.broadcasted_iota(jnp.int32, (8, LANES), 0)
    a_ref[0] = jnp.where(r8 == 0, pwr[c:c + 1], jnp.where(r8 == 1, pwi[c:c + 1], 0.0))


def s5_prep(lam_re, lam_im, log_dt, b_re, b_im, c_re, c_im, d):
    _, g, p = lam_re.shape
    h = S5_GROUP
    width = CHUNK * h
    pack = lambda a: jnp.concatenate([a[0], a[1]], axis=-1)
    lr = pack(lam_re)[:, None, :]
    li = pack(lam_im)[:, None, :]
    ldt = pack(jnp.broadcast_to(log_dt[..., None], (2, g, p)))[:, None, :]
    btr = pack(jnp.swapaxes(b_re, -1, -2))
    bti = pack(jnp.swapaxes(b_im, -1, -2))
    ctr = jnp.tile(jnp.swapaxes(c_re, -1, -2), (1, 1, 1, CHUNK))
    cti = jnp.tile(jnp.swapaxes(c_im, -1, -2), (1, 1, 1, CHUNK))
    row = lambda last: pl.BlockSpec((1, 1, last), lambda i: (i, 0, 0))
    col = lambda rows, last: pl.BlockSpec((2, 1, rows, last), lambda i: (0, i, 0, 0))
    return pl.pallas_call(
        _s5_prep_kernel,
        grid=(g,),
        in_specs=[row(LANES), row(LANES), row(LANES),
                  pl.BlockSpec((1, h, LANES), lambda i: (i, 0, 0)), pl.BlockSpec((1, h, LANES), lambda i: (i, 0, 0)),
                  col(p, 1), col(p, 1), col(1, 1), col(p, width), col(p, width),
                  pl.BlockSpec((1, h, 1), lambda i: (i, 0, 0))],
        out_specs=[pl.BlockSpec((1, width, 2 * width), lambda i: (i, 0, 0)),
                   pl.BlockSpec((1, width, width), lambda i: (i, 0, 0)),
                   pl.BlockSpec((1, 8, LANES), lambda i: (i, 0, 0))],
        out_shape=[jax.ShapeDtypeStruct((g, width, 2 * width), BF16),
                   jax.ShapeDtypeStruct((g, width, width), BF16),
                   jax.ShapeDtypeStruct((g, 8, LANES), F32)],
        compiler_params=_params("parallel"),
        name="s5_prep",
    )(lr, li, ldt, btr, bti, lam_re[..., None], lam_im[..., None], log_dt[..., None, None], ctr, cti,
      d.reshape(g, h, 1))


def _chunk_permutation():
    n, half = GROUPS_PER_LANE_BLOCK, CHUNK // 2
    pm = np.zeros((half * LANES, n * half * S5_GROUP), np.float32)
    for t in range(half):
        for g in range(n):
            for h in range(S5_GROUP):
                pm[t * LANES + g * S5_GROUP + h, g * LANES + t * S5_GROUP + h] = 1.0
    return pm


def _to_chunks_kernel(x_ref, p_ref, o_ref):
    cb, _, bsz, _ = x_ref.shape
    rows, half = cb * bsz, CHUNK // 2
    for hf in range(2):
        x8 = jnp.concatenate([x_ref[:, hf * half + t].reshape(rows, LANES) for t in range(half)], axis=1)
        out = _dot(x8, p_ref[...])
        for g in range(GROUPS_PER_LANE_BLOCK):
            o_ref[g, :, hf * LANES:(hf + 1) * LANES] = out[:, g * LANES:(g + 1) * LANES].astype(BF16)


def _from_chunks_kernel(y_ref, p_ref, o_ref):
    cb, _, bsz, _ = o_ref.shape
    half = CHUNK // 2
    for hf in range(2):
        y8 = jnp.concatenate([y_ref[g, :, hf * LANES:(hf + 1) * LANES] for g in range(GROUPS_PER_LANE_BLOCK)], axis=1)
        out = _dot_nt(y8, p_ref[...])
        for t in range(half):
            o_ref[:, hf * half + t] = out[:, t * LANES:(t + 1) * LANES].astype(BF16).reshape(cb, bsz, LANES)


def _chunk_relayout(kern, x, perm, out_struct, cb, to_chunks):
    n = GROUPS_PER_LANE_BLOCK
    if to_chunks:
        nc, _, bsz, d = x.shape
    else:
        nc, _, bsz, d = out_struct.shape
    tm_spec = pl.BlockSpec((cb, CHUNK, bsz, LANES), lambda i, j: (i, 0, 0, j))
    ch_spec = pl.BlockSpec((n, cb * bsz, CHUNK * S5_GROUP), lambda i, j: (j, i, 0))
    return pl.pallas_call(
        kern,
        grid=(nc // cb, d // LANES),
        in_specs=[tm_spec if to_chunks else ch_spec, _resident(perm.shape)],
        out_specs=ch_spec if to_chunks else tm_spec,
        out_shape=out_struct,
        compiler_params=_params("parallel", "parallel"),
        name="to_chunks" if to_chunks else "from_chunks",
    )(x, perm)


def _s5_chunk_kernel(u_ref, w1_ref, w2_ref, a_ref, y_ref, yint_ref, sin_ref, st_ref, *, rows_per_chunk):
    gb, nrows, width = u_ref.shape
    rb = rows_per_chunk
    n_chunks = nrows // rb
    ns = S5_STATE
    fwd = lax.broadcasted_iota(jnp.int32, (rb, LANES), 1) < ns

    for g in range(gb):
        r = _dot(u_ref[g], w1_ref[g])
        yint_ref[...] = r[:, :width]
        sin_ref[...] = r[:, width:]
        a_re = a_ref[g, 0:1, :]
        a_im = a_ref[g, 1:2, :]

        def step(i, carry):
            s_re, s_im = carry
            rf = pl.ds(pl.multiple_of(i * rb, rb), rb)
            rr = pl.ds(pl.multiple_of((n_chunks - 1 - i) * rb, rb), rb)
            st_ref[rf, 0:ns] = s_re[:, 0:ns]
            st_ref[rf, LANES:LANES + ns] = s_im[:, 0:ns]
            st_ref[rr, ns:LANES] = s_re[:, ns:LANES]
            st_ref[rr, LANES + ns:2 * LANES] = s_im[:, ns:LANES]
            in_re = jnp.where(fwd, sin_ref[rf, 0:LANES], sin_ref[rr, 0:LANES])
            in_im = jnp.where(fwd, sin_ref[rf, LANES:2 * LANES], sin_ref[rr, LANES:2 * LANES])
            return (a_re * s_re - a_im * s_im + in_re, a_re * s_im + a_im * s_re + in_im)

        zero = jnp.zeros((rb, LANES), F32)
        lax.fori_loop(0, n_chunks, step, (zero, zero), unroll=4)
        y_ref[g] = (yint_ref[...] + _dot(st_ref[...].astype(BF16), w2_ref[g])).astype(BF16)


def s5_chunk(u2, w1, w2, a, rows_per_chunk, gb):
    g, nrows, width = u2.shape
    kern = functools.partial(_s5_chunk_kernel, rows_per_chunk=rows_per_chunk)
    return pl.pallas_call(
        kern,
        grid=(g // gb,),
        in_specs=[pl.BlockSpec((gb, nrows, width), lambda i: (i, 0, 0)),
                  pl.BlockSpec((gb, width, 2 * width), lambda i: (i, 0, 0)),
                  pl.BlockSpec((gb, width, width), lambda i: (i, 0, 0)),
                  pl.BlockSpec((gb, 8, LANES), lambda i: (i, 0, 0))],
        out_specs=pl.BlockSpec((gb, nrows, width), lambda i: (i, 0, 0)),
        out_shape=jax.ShapeDtypeStruct((g, nrows, width), BF16),
        scratch_shapes=[pltpu.VMEM((nrows, width), F32), pltpu.VMEM((nrows, width), F32),
                        pltpu.VMEM((nrows, width), F32)],
        compiler_params=_params("parallel"),
        name="s5_chunk",
    )(u2, w1, w2, a)


def _diff_lambda(lq1_ref, lk1_ref, lq2_ref, lk2_ref, lambda_init):
    return (jnp.exp(jnp.sum(lq1_ref[...] * lk1_ref[...], axis=-1, keepdims=True))
            - jnp.exp(jnp.sum(lq2_ref[...] * lk2_ref[...], axis=-1, keepdims=True)) + lambda_init)


def _map_queries(q):
    lane = lax.broadcasted_iota(jnp.int32, q.shape, 1)
    zero = jnp.zeros_like(q)
    return jnp.where(lane < DIFF_DH, q, zero), jnp.where(lane < DIFF_DH, zero, q)


def _sublayer_norm_store(o, sg_ref, o_ref, lambda_init):
    on = o * lax.rsqrt(jnp.mean(o * o, axis=-1, keepdims=True) + EPS) * sg_ref[...]
    o_ref[0] = (on * (1.0 - lambda_init)).astype(BF16)


def _diff_attn_stream_kernel(q_ref, k_ref, v_ref, lq1_ref, lk1_ref, lq2_ref, lk2_ref, sg_ref, o_ref, e_ref, *,
                             lambda_init, tk, sub):
    tq = q_ref.shape[1]
    seq = k_ref.shape[1]
    lam = _diff_lambda(lq1_ref, lk1_ref, lq2_ref, lk2_ref, lambda_init)
    n_sub = tq // sub

    nkb = seq // tk

    def score_block(qmaps, h, kb, part):
        for mp in (0, 1):
            e = jnp.exp2(_dot_nt(qmaps[mp], k_ref[0, kb * tk:(kb + 1) * tk, :])).astype(BF16)
            e_ref[h % 2, mp, :, kb * tk:(kb + 1) * tk] = e
            for c in range(tk // LANES):
                part[mp] = part[mp] + e[:, c * LANES:(c + 1) * LANES].astype(F32)

    def value_block(h, kb, mu, acc):
        cols = []
        for c in range(tk // LANES):
            sl = slice(kb * tk + c * LANES, kb * tk + (c + 1) * LANES)
            cols.append(e_ref[h % 2, 0, :, sl] - mu * e_ref[h % 2, 1, :, sl])
        term = _dot(jnp.concatenate(cols, axis=1), v_ref[0, kb * tk:(kb + 1) * tk, :])
        return term if acc is None else acc + term

    prev = None
    for h in range(n_sub + 1):
        if h < n_sub:
            qmaps = _map_queries(q_ref[0, h * sub:(h + 1) * sub, :])
            part = [jnp.zeros((sub, LANES), F32), jnp.zeros((sub, LANES), F32)]
        acc = None
        for kb in range(nkb):
            if h < n_sub:
                score_block(qmaps, h, kb, part)
            if prev is not None:
                acc = value_block(h - 1, kb, prev[1], acc)
        if prev is not None:
            o = acc * (1.0 / prev[0])
            on = o * lax.rsqrt(jnp.mean(o * o, axis=-1, keepdims=True) + EPS) * sg_ref[...]
            o_ref[0, (h - 1) * sub:h * sub, :] = (on * (1.0 - lambda_init)).astype(BF16)
        if h < n_sub:
            l1, l2 = [jnp.sum(pt, axis=-1, keepdims=True) for pt in part]
            prev = (l1, jnp.broadcast_to(lam * l1 / l2, (sub, LANES)).astype(BF16))


def _diff_attn_exact_kernel(q_ref, k_ref, v_ref, lq1_ref, lk1_ref, lq2_ref, lk2_ref, sg_ref, o_ref, *, lambda_init):
    lam = _diff_lambda(lq1_ref, lk1_ref, lq2_ref, lk2_ref, lambda_init)
    k = k_ref[0]

    def softmax_parts(s):
        e = jnp.exp2(s - jnp.max(s, axis=-1, keepdims=True))
        return e, 1.0 / jnp.sum(e, axis=-1, keepdims=True)

    q1, q2 = _map_queries(q_ref[0])
    e1, r1 = softmax_parts(_dot_nt(q1, k))
    e2, r2 = softmax_parts(_dot_nt(q2, k))
    a = e1 * r1 - e2 * (r2 * lam)
    _sublayer_norm_store(_dot(a.astype(BF16), v_ref[0]), sg_ref, o_ref, lambda_init)


def diff_attn(q, k, v, lq1, lk1, lq2, lk2, sub_g, lambda_init, tq, streaming):
    b, seq, d = q.shape
    hw = 2 * DIFF_DH
    small = lambda n: pl.BlockSpec((1, n), lambda i, h, j: (0, 0))
    scratch = []
    if streaming:
        sub = tq
        tq = min(ATTN_SUBTILES * sub, seq)
        kern = functools.partial(_diff_attn_stream_kernel, lambda_init=lambda_init, tk=min(ATTN_KEY_BLOCK, seq), sub=sub)
        scratch = [pltpu.VMEM((2, 2, sub, seq), BF16)]
    else:
        kern = functools.partial(_diff_attn_exact_kernel, lambda_init=lambda_init)
    return pl.pallas_call(
        kern,
        grid=(b, DIFF_HEADS, seq // tq),
        in_specs=[pl.BlockSpec((1, tq, hw), lambda i, h, j: (i, j, h)),
                  pl.BlockSpec((1, seq, hw), lambda i, h, j: (i, 0, h)),
                  pl.BlockSpec((1, seq, hw), lambda i, h, j: (i, 0, h)),
                  small(DIFF_DH), small(DIFF_DH), small(DIFF_DH), small(DIFF_DH), small(hw)],
        out_specs=pl.BlockSpec((1, tq, hw), lambda i, h, j: (i, j, h)),
        out_shape=jax.ShapeDtypeStruct((b, seq, d), BF16),
        scratch_shapes=scratch,
        compiler_params=_params("parallel", "parallel", "parallel"),
        name="diff_attn_stream" if streaming else "diff_attn_exact",
    )(q, k, v, lq1, lk1, lq2, lk2, sub_g)


def _mem_kv_kernel(m_ref, g_ref, w_ref, kg_ref, k_ref, v_ref):
    d = m_ref.shape[-1]
    m = m_ref[0]
    mn = (m * lax.rsqrt(jnp.mean(m * m, axis=-1, keepdims=True) + EPS) * g_ref[...]).astype(BF16)
    kv = _dot(mn, w_ref[...])
    mdh = d // MEM_HEADS
    for h in range(MEM_HEADS):
        seg = kv[:, h * mdh:(h + 1) * mdh]
        segn = seg * lax.rsqrt(jnp.mean(seg * seg, axis=-1, keepdims=True) + EPS) * kg_ref[:, h * mdh:(h + 1) * mdh]
        k_ref[0, :, h * mdh:(h + 1) * mdh] = segn.astype(BF16)
    v_ref[0] = kv[:, d:].astype(BF16)


def mem_kv(mem, g, w_kv, kg):
    b, m, d = mem.shape
    blk = pl.BlockSpec((1, m, d), lambda i: (i, 0, 0))
    return pl.pallas_call(
        _mem_kv_kernel,
        grid=(b,),
        in_specs=[blk, _resident((1, d)), _resident((d, 2 * d)), _resident((1, d))],
        out_specs=[blk, blk],
        out_shape=[jax.ShapeDtypeStruct((b, m, d), BF16)] * 2,
        compiler_params=_params("parallel"),
        name="mem_kv",
    )(mem, g, w_kv, kg)


def _merge_kernel(x_ref, y_ref, do_ref, qm_ref, gt_ref, mk_ref, mv_ref, wglu_ref, wb_ref, wo_ref, o_ref):
    d = x_ref.shape[-1]
    mdh = d // MEM_HEADS
    heads = [slice(h * mdh, (h + 1) * mdh) for h in range(MEM_HEADS)]

    def gated(n, br):
        return gt_ref[0, :, n * d:(n + 1) * d].astype(F32) * _dot(br, wb_ref[n])

    merged = gated(1, do_ref[0])
    qm = qm_ref[0]
    mk = mk_ref[0]
    scores = [_dot_nt(qm[:, sl], mk[:, sl]) for sl in heads]

    z = jax.nn.gelu(y_ref[0].astype(F32))
    s5o = (z * jax.nn.sigmoid(_dot(z.astype(BF16), wglu_ref[...]))).astype(BF16)

    mv = mv_ref[0]
    mem_parts = []
    for s, sl in zip(scores, heads):
        e = jnp.exp(s - jnp.max(s, axis=-1, keepdims=True))
        pr = e * (1.0 / jnp.sum(e, axis=-1, keepdims=True))
        mem_parts.append(_dot(pr.astype(BF16), mv[:, sl]).astype(BF16))
    mo = jnp.concatenate(mem_parts, axis=-1)

    merged = merged + gated(0, s5o) + gated(2, mo)
    o_ref[0] = x_ref[0] + _dot(merged.astype(BF16), wo_ref[...])


def merge(x, y, do, qm, gates, mk, mv, w_glu, w_branch, w_out, tt):
    b, seq, d = x.shape
    m = mk.shape[1]
    tok = lambda width: pl.BlockSpec((1, tt, width), lambda i, j: (i, j, 0))
    memb = pl.BlockSpec((1, m, d), lambda i, j: (i, 0, 0))
    return pl.pallas_call(
        _merge_kernel,
        grid=(b, seq // tt),
        in_specs=[tok(d), tok(d), tok(d), tok(d), tok(3 * d), memb, memb,
                  _resident((d, d)), _resident((3, d, d)), _resident((d, d))],
        out_specs=tok(d),
        out_shape=jax.ShapeDtypeStruct((b, seq, d), F32),
        compiler_params=_params("parallel", "parallel"),
        name="merge",
    )(x, y, do, qm, gates, mk, mv, w_glu, w_branch, w_out)


def _ffn_kernel(x_ref, g_ref, wgu_ref, wd_ref, o_ref):
    dff = wd_ref.shape[0]
    rows = x_ref.shape[1]
    half = rows // 2 if rows % 512 == 0 else rows
    for r0 in range(0, rows, half):
        x = x_ref[0, r0:r0 + half, :]
        hn = (x * lax.rsqrt(jnp.mean(x * x, axis=-1, keepdims=True) + EPS) * g_ref[...]).astype(BF16)
        gate = _dot(hn, wgu_ref[:, :dff])
        up = _dot(hn, wgu_ref[:, dff:])
        act = (jax.nn.silu(gate) * up).astype(BF16)
        o_ref[0, r0:r0 + half, :] = x + _dot(act, wd_ref[...])


def ffn(x, g, w_gate_up, w_down, tt):
    b, seq, d = x.shape
    dff = w_down.shape[0]
    tok = pl.BlockSpec((1, tt, d), lambda i, j: (i, j, 0))
    return pl.pallas_call(
        _ffn_kernel,
        grid=(b, seq // tt),
        in_specs=[tok, _resident((1, d)), _resident((d, 2 * dff)), _resident((dff, d))],
        out_specs=tok,
        out_shape=jax.ShapeDtypeStruct((b, seq, d), F32),
        compiler_params=_params("parallel", "parallel"),
        name="ffn",
    )(x, g, w_gate_up, w_down)


def _s5_mixer(u, s5_ops, perm):
    b, seq, d = u.shape
    nc = seq // CHUNK
    cb = min(S5_CHUNKS_PER_STEP, nc)
    w1, w2, a = s5_ops
    u_tm = jnp.swapaxes(u, 0, 1).reshape(nc, CHUNK, b, d)
    chunk_rows = jax.ShapeDtypeStruct((d // S5_GROUP, nc * b, CHUNK * S5_GROUP), BF16)
    u2 = _chunk_relayout(_to_chunks_kernel, u_tm, perm, chunk_rows, cb, True)
    y2 = s5_chunk(u2, w1, w2, a, b, S5_GROUPS_PER_STEP)
    y_tm = _chunk_relayout(_from_chunks_kernel, y2, perm, jax.ShapeDtypeStruct(u_tm.shape, BF16), cb, False)
    return jnp.swapaxes(y_tm.reshape(seq, b, d), 0, 1)


def _layer(x, mem, p, li, s5_ops, tables, perm):
    b, seq, d = x.shape
    tt = min(DENSE_TOKENS, seq)
    tq = min(ATTN_QUERY_ROWS, seq)
    row = lambda a: a.reshape(1, -1)
    tile_to_d = lambda a: jnp.tile(a, d // a.shape[0]).reshape(1, d)
    head_of_lane = (lax.broadcasted_iota(jnp.int32, (d, LANES), 0) // DIFF_DH
                    == lax.broadcasted_iota(jnp.int32, (d, LANES), 1))
    hsel = jnp.where(head_of_lane, 1.0 / DIFF_DH, 0.0).astype(BF16)
    hexp = head_of_lane.T.astype(BF16)
    cos, sin = tables

    u, q, k, v, qm, gates = proj(x, row(p["norm_mix_g"][li]), p["w_in"][li].astype(BF16), hsel, hexp, row(p["b_gate"][li]),
                                 tile_to_d(p["diff_q_g"][li]), tile_to_d(p["diff_k_g"][li]), tile_to_d(p["mem_q_g"][li]),
                                 cos[:seq], sin[:seq], tt)

    y = _s5_mixer(u, s5_ops, perm)

    lambda_init = 0.8 - 0.6 * math.exp(-0.3 * li)
    attn_args = (q, k, v, row(p["diff_lq1"][li]), row(p["diff_lk1"][li]), row(p["diff_lq2"][li]),
                 row(p["diff_lk2"][li]), row(p["diff_sub_g"][li]))
    score_bound = (1.05 * DIFF_DH ** 0.5 * LOG2E) * jnp.max(jnp.abs(p["diff_q_g"][li])) * jnp.max(jnp.abs(p["diff_k_g"][li]))
    do = lax.cond(score_bound <= SCORE_BOUND,
                  lambda *a: diff_attn(*a, lambda_init, tq, True),
                  lambda *a: diff_attn(*a, lambda_init, tq, False), *attn_args)

    mk, mv = mem_kv(mem, row(p["mem_norm_g"][li]), p["w_mem_kv"][li].astype(BF16), tile_to_d(p["mem_k_g"][li]))
    x1 = merge(x, y, do, qm, gates, mk, mv, p["s5_w_glu"][li].astype(BF16), p["w_branch"][li].astype(BF16),
               p["w_out"][li].astype(BF16), tt)
    return ffn(x1, row(p["ffn_norm_g"][li]), p["w_gate_up"][li].astype(BF16), p["w_down"][li].astype(BF16), tt)


def kernel(x_prompt, x_sample, mem_prompt, mem_sample, norm_mix_g, w_in, b_gate, s5_lam_re, s5_lam_im, s5_log_dt, s5_b_re, s5_b_im, s5_c_re, s5_c_im, s5_d, s5_w_glu, diff_q_g, diff_k_g, diff_lq1, diff_lk1, diff_lq2, diff_lk2, diff_sub_g, mem_norm_g, w_mem_kv, mem_q_g, mem_k_g, w_branch, w_out, ffn_norm_g, w_gate_up, w_down):
    p = dict(norm_mix_g=norm_mix_g, w_in=w_in, b_gate=b_gate, s5_w_glu=s5_w_glu, diff_q_g=diff_q_g, diff_k_g=diff_k_g,
             diff_lq1=diff_lq1, diff_lk1=diff_lk1, diff_lq2=diff_lq2, diff_lk2=diff_lk2, diff_sub_g=diff_sub_g,
             mem_norm_g=mem_norm_g, w_mem_kv=w_mem_kv, mem_q_g=mem_q_g, mem_k_g=mem_k_g, w_branch=w_branch,
             w_out=w_out, ffn_norm_g=ffn_norm_g, w_gate_up=w_gate_up, w_down=w_down)
    tables = rope_tables(max(x_prompt.shape[1], x_sample.shape[1]))
    perm = jnp.asarray(_chunk_permutation(), BF16)
    y_prompt, y_sample = x_prompt, x_sample
    for li in range(norm_mix_g.shape[0]):
        s5_ops = s5_prep(s5_lam_re[li], s5_lam_im[li], s5_log_dt[li], s5_b_re[li], s5_b_im[li],
                         s5_c_re[li], s5_c_im[li], s5_d[li])
        y_prompt = _layer(y_prompt, mem_prompt, p, li, s5_ops, tables, perm)
        y_sample = _layer(y_sample, mem_sample, p, li, s5_ops, tables, perm)
    return (y_prompt, y_sample)
```

```python
import functools
import math

import numpy as np
import jax
import jax.numpy as jnp
from jax import lax
from jax.experimental import pallas as pl
from jax.experimental.pallas import tpu as pltpu

F32 = jnp.float32
BF16 = jnp.bfloat16
EPS = 1e-6
ROPE_THETA = 10000.0
LOG2E = math.log2(math.e)

S5_GROUP = 16
S5_STATE = 64
CHUNK = 16
DIFF_HEADS = 8
DIFF_DH = 64
MEM_HEADS = 4
LANES = 128
GROUPS_PER_LANE_BLOCK = LANES // S5_GROUP
VMEM_LIMIT = 56 * 1024 * 1024
DENSE_TOKENS = 512
ATTN_QUERY_ROWS = 256
ATTN_SUBTILES = 8
ATTN_KEY_BLOCK = 512
S5_CHUNKS_PER_STEP = 64
S5_GROUPS_PER_STEP = 2
SCORE_BOUND = 60.0

HIGHEST = lax.Precision.HIGHEST


def _dot(a, b):
    return jnp.dot(a, b, preferred_element_type=F32)


def _dot_nt(a, b):
    return lax.dot_general(a, b, (((1,), (1,)), ((), ())), preferred_element_type=F32)


def _dot_hi(a, b):
    return jnp.dot(a, b, precision=HIGHEST, preferred_element_type=F32)


def _params(*sem):
    return pltpu.CompilerParams(dimension_semantics=sem, vmem_limit_bytes=VMEM_LIMIT)


def _resident(shape):
    nd = len(shape)
    return pl.BlockSpec(shape, lambda *_: (0,) * nd, pipeline_mode=pl.Buffered(1))


def _rope_kernel(cos_ref, sin_ref):
    rows = cos_ref.shape[0]
    base = pl.program_id(0) * rows
    pos = (lax.broadcasted_iota(jnp.int32, (rows, LANES), 0) + base).astype(F32)
    lane = lax.broadcasted_iota(jnp.int32, (rows, LANES), 1)
    half = DIFF_DH // 2
    freq = (lane % half).astype(F32)
    inv = jnp.exp(freq * (-math.log(ROPE_THETA) / half))
    ang = pos * inv
    cos_ref[...] = jnp.cos(ang)
    s = jnp.sin(ang)
    sin_ref[...] = jnp.where((lane % DIFF_DH) < half, -s, s)


def rope_tables(seq):
    rows = min(seq, 512)
    return pl.pallas_call(
        _rope_kernel,
        grid=(seq // rows,),
        out_specs=[pl.BlockSpec((rows, LANES), lambda i: (i, 0))] * 2,
        out_shape=[jax.ShapeDtypeStruct((seq, LANES), F32)] * 2,
        name="rope_tables",
    )()


def _proj_kernel(x_ref, g_ref, w_ref, hsel_ref, hexp_ref, bg_ref, qg_ref, kg_ref, mqg_ref, cos_ref, sin_ref,
                 u_ref, q_ref, k_ref, v_ref, qm_ref, gt_ref):
    rows, d = x_ref.shape[1:]
    half = rows // 2 if rows % 512 == 0 else rows
    for r0 in range(0, rows, half):
        _proj_rows(slice(r0, r0 + half), d, x_ref, g_ref, w_ref, hsel_ref, hexp_ref, bg_ref, qg_ref, kg_ref, mqg_ref,
                   cos_ref, sin_ref, u_ref, q_ref, k_ref, v_ref, qm_ref, gt_ref)


def _proj_rows(rs, d, x_ref, g_ref, w_ref, hsel_ref, hexp_ref, bg_ref, qg_ref, kg_ref, mqg_ref, cos_ref, sin_ref,
               u_ref, q_ref, k_ref, v_ref, qm_ref, gt_ref):
    x = x_ref[0, rs, :]
    hn = (x * lax.rsqrt(jnp.mean(x * x, axis=-1, keepdims=True) + EPS) * g_ref[...]).astype(BF16)

    def mm(j):
        return _dot(hn, w_ref[:, j * d:(j + 1) * d])

    cos = cos_ref[rs, :]
    sin = sin_ref[rs, :]
    lane = lax.broadcasted_iota(jnp.int32, cos.shape, 1)
    first_half = (lane % DIFF_DH) < (DIFF_DH // 2)

    def head_sums(t):
        return _dot((t * t).astype(BF16), hsel_ref[...]).astype(BF16)

    def rope_store(t, sums, gain, scale, out_ref):
        tn = t * lax.rsqrt(_dot(sums, hexp_ref[...]) + EPS) * gain
        for j in range(d // LANES):
            blk = tn[:, j * LANES:(j + 1) * LANES]
            partner = jnp.where(first_half, pltpu.roll(blk, LANES - DIFF_DH // 2, 1), pltpu.roll(blk, DIFF_DH // 2, 1))
            out_ref[0, rs, j * LANES:(j + 1) * LANES] = ((blk * cos + partner * sin) * scale).astype(BF16)

    def gate_store(j, gl):
        gt_ref[0, rs, j * d:(j + 1) * d] = jax.nn.sigmoid(gl + bg_ref[:, j * d:(j + 1) * d]).astype(BF16)

    q = mm(1)
    k = mm(2)
    q_sums = head_sums(q)
    gl0 = mm(5)
    k_sums = head_sums(k)
    gl1 = mm(6)
    rope_store(q, q_sums, qg_ref[...], DIFF_DH ** -0.5 * LOG2E, q_ref)
    gl2 = mm(7)
    rope_store(k, k_sums, kg_ref[...], 1.0, k_ref)
    qm = mm(4)
    gate_store(0, gl0)
    gate_store(1, gl1)
    u = mm(0)
    gate_store(2, gl2)
    mdh = d // MEM_HEADS
    for h in range(MEM_HEADS):
        seg = qm[:, h * mdh:(h + 1) * mdh]
        segn = seg * lax.rsqrt(jnp.mean(seg * seg, axis=-1, keepdims=True) + EPS) * mqg_ref[:, h * mdh:(h + 1) * mdh]
        qm_ref[0, rs, h * mdh:(h + 1) * mdh] = (segn * (mdh ** -0.5)).astype(BF16)
    v = mm(3)
    u_ref[0, rs, :] = u.astype(BF16)
    v_ref[0, rs, :] = v.astype(BF16)


def proj(x, g, w_in, hsel, hexp, b_gate, qg, kg, mqg, cos, sin, tt):
    b, seq, d = x.shape
    ncol = w_in.shape[1]
    tok = lambda width: pl.BlockSpec((1, tt, width), lambda i, j: (i, j, 0))
    outs = [d, d, d, d, d, 3 * d]
    return pl.pallas_call(
        _proj_kernel,
        grid=(b, seq // tt),
        in_specs=[tok(d), _resident((1, d)), _resident((d, ncol)), _resident((d, LANES)), _resident((LANES, d)),
                  _resident((1, 3 * d)),
                  _resident((1, d)), _resident((1, d)), _resident((1, d)),
                  pl.BlockSpec((tt, LANES), lambda i, j: (j, 0)), pl.BlockSpec((tt, LANES), lambda i, j: (j, 0))],
        out_specs=[tok(w) for w in outs],
        out_shape=[jax.ShapeDtypeStruct((b, seq, w), BF16) for w in outs],
        compiler_params=_params("parallel", "parallel"),
        name="proj",
    )(x, g, w_in, hsel, hexp, b_gate, qg, kg, mqg, cos, sin)


def _s5_prep_kernel(lr_ref, li_ref, ldt_ref, btr_ref, bti_ref, lcr_ref, lci_ref, ldtc_ref, ctr_ref, cti_ref, d_ref,
                    w1_ref, w2_ref, a_ref):
    p, h, c = S5_STATE, S5_GROUP, CHUNK
    width = c * h
    lane128 = lax.broadcasted_iota(jnp.int32, (1, LANES), 1)
    fwd128 = lane128 < p

    dt = jnp.exp(ldt_ref[0])
    zr, zi = lr_ref[0] * dt, li_ref[0] * dt
    kk = jnp.minimum(lax.broadcasted_iota(jnp.int32, (24, 1), 0), c).astype(F32)
    mag = jnp.exp(kk * zr)
    pwr, pwi = mag * jnp.cos(kk * zi), mag * jnp.sin(kk * zi)

    lr, li = lr_ref[0], li_ref[0]
    nr, ni = pwr[1:2] - 1.0, pwi[1:2]
    den = lr * lr + li * li
    cr, ci = (nr * lr + ni * li) / den, (ni * lr - nr * li) / den
    btr, bti = btr_ref[0], bti_ref[0]
    bbr, bbi = cr * btr - ci * bti, cr * bti + ci * btr

    for s in range(c):
        sr = jnp.where(fwd128, pwr[c - 1 - s:c - s], pwr[s:s + 1])
        si = jnp.where(fwd128, pwi[c - 1 - s:c - s], pwi[s:s + 1])
        w1_ref[0, s * h:(s + 1) * h, width:width + LANES] = (bbr * sr - bbi * si).astype(BF16)
        w1_ref[0, s * h:(s + 1) * h, width + LANES:width + 2 * LANES] = (bbr * si + bbi * sr).astype(BF16)

    krow = jnp.minimum(lax.broadcasted_iota(jnp.int32, (1, LANES), 1), c).astype(F32)
    sel_k = lax.broadcasted_iota(jnp.int32, (LANES, width), 0)
    sel_t = lax.broadcasted_iota(jnp.int32, (LANES, width), 1) // h

    def expand(pw, power_of_t):
        return _dot_hi(pw, (sel_k == power_of_t(sel_t)).astype(F32))

    zr_stack, zi_stack = [], []
    for dr in range(2):
        dtc = jnp.exp(ldtc_ref[dr, 0])
        zrc, zic = lcr_ref[dr, 0] * dtc, lci_ref[dr, 0] * dtc
        magc = jnp.exp(zrc * krow)
        pcr, pci = magc * jnp.cos(zic * krow), magc * jnp.sin(zic * krow)
        ctr, cti = ctr_ref[dr, 0], cti_ref[dr, 0]
        lag = (lambda t: t) if dr == 0 else (lambda t: c - 1 - t)
        er, ei = expand(pcr, lag), expand(pci, lag)
        zr_stack.append(er * ctr - ei * cti)
        zi_stack.append(er * cti + ei * ctr)
        rd = (lambda t: t + 1) if dr == 0 else (lambda t: c - t)
        er, ei = expand(pcr, rd), expand(pci, rd)
        re_rows = (er * ctr - ei * cti).astype(BF16)
        im_rows = (-(er * cti + ei * ctr)).astype(BF16)
        w2_ref[0, dr * p:(dr + 1) * p] = re_rows
        w2_ref[0, LANES + dr * p:LANES + (dr + 1) * p] = im_rows

    zr_all = jnp.concatenate(zr_stack, axis=0)
    zi_all = jnp.concatenate(zi_stack, axis=0)
    lane16 = lax.broadcasted_iota(jnp.int32, (h, LANES), 1)
    g_f = _dot_hi(jnp.where(lane16 < p, bbr, 0.0), zr_all) - _dot_hi(jnp.where(lane16 < p, bbi, 0.0), zi_all)
    g_b = _dot_hi(jnp.where(lane16 < p, 0.0, bbr), zr_all) - _dot_hi(jnp.where(lane16 < p, 0.0, bbi), zi_all)

    lane = lax.broadcasted_iota(jnp.int32, (h, width), 1)
    row = lax.broadcasted_iota(jnp.int32, (h, width), 0)
    dcol = d_ref[0]
    for s in range(c):
        m_f = jnp.where(lane >= h * s, g_f if s == 0 else pltpu.roll(g_f, h * s, 1), 0.0)
        m_b = jnp.where(lane < h * (s + 1), g_b if s == c - 1 else pltpu.roll(g_b, h * (s + 1), 1), 0.0)
        skip = jnp.where(lane == h * s + row, dcol, 0.0)
        w1_ref[0, s * h:(s + 1) * h, 0:width] = (m_f + m_b + skip).astype(BF16)

    r8 = lax.broadcasted_iota(jnp.int32, (8, LANES), 0)
    a_ref[0] = jnp.where(r8 == 0, pwr[c:c + 1], jnp.where(r8 == 1, pwi[c:c + 1], 0.0))


def s5_prep(lam_re, lam_im, log_dt, b_re, b_im, c_re, c_im, d):
    _, g, p = lam_re.shape
    h = S5_GROUP
    width = CHUNK * h
    pack = lambda a: jnp.concatenate([a[0], a[1]], axis=-1)
    lr = pack(lam_re)[:, None, :]
    li = pack(lam_im)[:, None, :]
    ldt = pack(jnp.broadcast_to(log_dt[..., None], (2, g, p)))[:, None, :]
    btr = pack(jnp.swapaxes(b_re, -1, -2))
    bti = pack(jnp.swapaxes(b_im, -1, -2))
    ctr = jnp.tile(jnp.swapaxes(c_re, -1, -2), (1, 1, 1, CHUNK))
    cti = jnp.tile(jnp.swapaxes(c_im, -1, -2), (1, 1, 1, CHUNK))
    row = lambda last: pl.BlockSpec((1, 1, last), lambda i: (i, 0, 0))
    col = lambda rows, last: pl.BlockSpec((2, 1, rows, last), lambda i: (0, i, 0, 0))
    return pl.pallas_call(
        _s5_prep_kernel,
        grid=(g,),
        in_specs=[row(LANES), row(LANES), row(LANES),
                  pl.BlockSpec((1, h, LANES), lambda i: (i, 0, 0)), pl.BlockSpec((1, h, LANES), lambda i: (i, 0, 0)),
                  col(p, 1), col(p, 1), col(1, 1), col(p, width), col(p, width),
                  pl.BlockSpec((1, h, 1), lambda i: (i, 0, 0))],
        out_specs=[pl.BlockSpec((1, width, 2 * width), lambda i: (i, 0, 0)),
                   pl.BlockSpec((1, width, width), lambda i: (i, 0, 0)),
                   pl.BlockSpec((1, 8, LANES), lambda i: (i, 0, 0))],
        out_shape=[jax.ShapeDtypeStruct((g, width, 2 * width), BF16),
                   jax.ShapeDtypeStruct((g, width, width), BF16),
                   jax.ShapeDtypeStruct((g, 8, LANES), F32)],
        compiler_params=_params("parallel"),
        name="s5_prep",
    )(lr, li, ldt, btr, bti, lam_re[..., None], lam_im[..., None], log_dt[..., None, None], ctr, cti,
      d.reshape(g, h, 1))


def _chunk_permutation():
    n, half = GROUPS_PER_LANE_BLOCK, CHUNK // 2
    pm = np.zeros((half * LANES, n * half * S5_GROUP), np.float32)
    for t in range(half):
        for g in range(n):
            for h in range(S5_GROUP):
                pm[t * LANES + g * S5_GROUP + h, g * LANES + t * S5_GROUP + h] = 1.0
    return pm


def _to_chunks_kernel(x_ref, p_ref, o_ref):
    cb, _, bsz, _ = x_ref.shape
    rows, half = cb * bsz, CHUNK // 2
    for hf in range(2):
        x8 = jnp.concatenate([x_ref[:, hf * half + t].reshape(rows, LANES) for t in range(half)], axis=1)
        out = _dot(x8, p_ref[...])
        for g in range(GROUPS_PER_LANE_BLOCK):
            o_ref[g, :, hf * LANES:(hf + 1) * LANES] = out[:, g * LANES:(g + 1) * LANES].astype(BF16)


def _from_chunks_kernel(y_ref, p_ref, o_ref):
    cb, _, bsz, _ = o_ref.shape
    half = CHUNK // 2
    for hf in range(2):
        y8 = jnp.concatenate([y_ref[g, :, hf * LANES:(hf + 1) * LANES] for g in range(GROUPS_PER_LANE_BLOCK)], axis=1)
        out = _dot_nt(y8, p_ref[...])
        for t in range(half):
            o_ref[:, hf * half + t] = out[:, t * LANES:(t + 1) * LANES].astype(BF16).reshape(cb, bsz, LANES)


def _chunk_relayout(kern, x, perm, out_struct, cb, to_chunks):
    n = GROUPS_PER_LANE_BLOCK
    if to_chunks:
        nc, _, bsz, d = x.shape
    else:
        nc, _, bsz, d = out_struct.shape
    tm_spec = pl.BlockSpec((cb, CHUNK, bsz, LANES), lambda i, j: (i, 0, 0, j))
    ch_spec = pl.BlockSpec((n, cb * bsz, CHUNK * S5_GROUP), lambda i, j: (j, i, 0))
    return pl.pallas_call(
        kern,
        grid=(nc // cb, d // LANES),
        in_specs=[tm_spec if to_chunks else ch_spec, _resident(perm.shape)],
        out_specs=ch_spec if to_chunks else tm_spec,
        out_shape=out_struct,
        compiler_params=_params("parallel", "parallel"),
        name="to_chunks" if to_chunks else "from_chunks",
    )(x, perm)


def _s5_chunk_kernel(u_ref, w1_ref, w2_ref, a_ref, y_ref, yint_ref, sin_ref, st_ref, *, rows_per_chunk):
    gb, nrows, width = u_ref.shape
    rb = rows_per_chunk
    n_chunks = nrows // rb
    ns = S5_STATE
    fwd = lax.broadcasted_iota(jnp.int32, (rb, LANES), 1) < ns

    for g in range(gb):
        r = _dot(u_ref[g], w1_ref[g])
        yint_ref[...] = r[:, :width]
        sin_ref[...] = r[:, width:]
        a_re = a_ref[g, 0:1, :]
        a_im = a_ref[g, 1:2, :]

        def step(i, carry):
            s_re, s_im = carry
            rf = pl.ds(pl.multiple_of(i * rb, rb), rb)
            rr = pl.ds(pl.multiple_of((n_chunks - 1 - i) * rb, rb), rb)
            st_ref[rf, 0:ns] = s_re[:, 0:ns]
            st_ref[rf, LANES:LANES + ns] = s_im[:, 0:ns]
            st_ref[rr, ns:LANES] = s_re[:, ns:LANES]
            st_ref[rr, LANES + ns:2 * LANES] = s_im[:, ns:LANES]
            in_re = jnp.where(fwd, sin_ref[rf, 0:LANES], sin_ref[rr, 0:LANES])
            in_im = jnp.where(fwd, sin_ref[rf, LANES:2 * LANES], sin_ref[rr, LANES:2 * LANES])
            return (a_re * s_re - a_im * s_im + in_re, a_re * s_im + a_im * s_re + in_im)

        zero = jnp.zeros((rb, LANES), F32)
        lax.fori_loop(0, n_chunks, step, (zero, zero), unroll=4)
        y_ref[g] = (yint_ref[...] + _dot(st_ref[...].astype(BF16), w2_ref[g])).astype(BF16)


def s5_chunk(u2, w1, w2, a, rows_per_chunk, gb):
    g, nrows, width = u2.shape
    kern = functools.partial(_s5_chunk_kernel, rows_per_chunk=rows_per_chunk)
    return pl.pallas_call(
        kern,
        grid=(g // gb,),
        in_specs=[pl.BlockSpec((gb, nrows, width), lambda i: (i, 0, 0)),
                  pl.BlockSpec((gb, width, 2 * width), lambda i: (i, 0, 0)),
                  pl.BlockSpec((gb, width, width), lambda i: (i, 0, 0)),
                  pl.BlockSpec((gb, 8, LANES), lambda i: (i, 0, 0))],
        out_specs=pl.BlockSpec((gb, nrows, width), lambda i: (i, 0, 0)),
        out_shape=jax.ShapeDtypeStruct((g, nrows, width), BF16),
        scratch_shapes=[pltpu.VMEM((nrows, width), F32), pltpu.VMEM((nrows, width), F32),
                        pltpu.VMEM((nrows, width), F32)],
        compiler_params=_params("parallel"),
        name="s5_chunk",
    )(u2, w1, w2, a)


def _diff_lambda(lq1_ref, lk1_ref, lq2_ref, lk2_ref, lambda_init):
    return (jnp.exp(jnp.sum(lq1_ref[...] * lk1_ref[...], axis=-1, keepdims=True))
            - jnp.exp(jnp.sum(lq2_ref[...] * lk2_ref[...], axis=-1, keepdims=True)) + lambda_init)


def _map_queries(q):
    lane = lax.broadcasted_iota(jnp.int32, q.shape, 1)
    zero = jnp.zeros_like(q)
    return jnp.where(lane < DIFF_DH, q, zero), jnp.where(lane < DIFF_DH, zero, q)


def _sublayer_norm_store(o, sg_ref, o_ref, lambda_init):
    on = o * lax.rsqrt(jnp.mean(o * o, axis=-1, keepdims=True) + EPS) * sg_ref[...]
    o_ref[0] = (on * (1.0 - lambda_init)).astype(BF16)


def _diff_attn_stream_kernel(q_ref, k_ref, v_ref, lq1_ref, lk1_ref, lq2_ref, lk2_ref, sg_ref, o_ref, e_ref, *,
                             lambda_init, tk, sub):
    tq = q_ref.shape[1]
    seq = k_ref.shape[1]
    lam = _diff_lambda(lq1_ref, lk1_ref, lq2_ref, lk2_ref, lambda_init)
    n_sub = tq // sub

    nkb = seq // tk

    def score_block(qmaps, h, kb, part):
        for mp in (0, 1):
            e = jnp.exp2(_dot_nt(qmaps[mp], k_ref[0, kb * tk:(kb + 1) * tk, :])).astype(BF16)
            e_ref[h % 2, mp, :, kb * tk:(kb + 1) * tk] = e
            cols = [e[:, c * LANES:(c + 1) * LANES] for c in range(tk // LANES)]
            while len(cols) > 1:
                cols = [cols[i] + cols[i + 1] for i in range(0, len(cols), 2)]
            part[mp] = part[mp] + cols[0].astype(F32)

    def value_block(h, kb, mu, acc):
        cols = []
        for c in range(tk // LANES):
            sl = slice(kb * tk + c * LANES, kb * tk + (c + 1) * LANES)
            cols.append(e_ref[h % 2, 0, :, sl] - mu * e_ref[h % 2, 1, :, sl])
        term = _dot(jnp.concatenate(cols, axis=1), v_ref[0, kb * tk:(kb + 1) * tk, :])
        return term if acc is None else acc + term

    prev = None
    for h in range(n_sub + 1):
        if h < n_sub:
            qmaps = _map_queries(q_ref[0, h * sub:(h + 1) * sub, :])
            part = [jnp.zeros((sub, LANES), F32), jnp.zeros((sub, LANES), F32)]
        acc = None
        for kb in range(nkb):
            if h < n_sub:
                score_block(qmaps, h, kb, part)
            if prev is not None:
                acc = value_block(h - 1, kb, prev[1], acc)
        if prev is not None:
            o = acc * (1.0 / prev[0])
            on = o * lax.rsqrt(jnp.mean(o * o, axis=-1, keepdims=True) + EPS) * sg_ref[...]
            o_ref[0, (h - 1) * sub:h * sub, :] = (on * (1.0 - lambda_init)).astype(BF16)
        if h < n_sub:
            l1, l2 = [jnp.sum(pt, axis=-1, keepdims=True) for pt in part]
            prev = (l1, jnp.broadcast_to(lam * l1 / l2, (sub, LANES)).astype(BF16))


def _diff_attn_exact_kernel(q_ref, k_ref, v_ref, lq1_ref, lk1_ref, lq2_ref, lk2_ref, sg_ref, o_ref, *, lambda_init):
    lam = _diff_lambda(lq1_ref, lk1_ref, lq2_ref, lk2_ref, lambda_init)
    k = k_ref[0]

    def softmax_parts(s):
        e = jnp.exp2(s - jnp.max(s, axis=-1, keepdims=True))
        return e, 1.0 / jnp.sum(e, axis=-1, keepdims=True)

    q1, q2 = _map_queries(q_ref[0])
    e1, r1 = softmax_parts(_dot_nt(q1, k))
    e2, r2 = softmax_parts(_dot_nt(q2, k))
    a = e1 * r1 - e2 * (r2 * lam)
    _sublayer_norm_store(_dot(a.astype(BF16), v_ref[0]), sg_ref, o_ref, lambda_init)


def diff_attn(q, k, v, lq1, lk1, lq2, lk2, sub_g, lambda_init, tq, streaming):
    b, seq, d = q.shape
    hw = 2 * DIFF_DH
    small = lambda n: pl.BlockSpec((1, n), lambda i, h, j: (0, 0))
    scratch = []
    if streaming:
        sub = tq
        tq = min(ATTN_SUBTILES * sub, seq)
        kern = functools.partial(_diff_attn_stream_kernel, lambda_init=lambda_init, tk=min(ATTN_KEY_BLOCK, seq), sub=sub)
        scratch = [pltpu.VMEM((2, 2, sub, seq), BF16)]
    else:
        kern = functools.partial(_diff_attn_exact_kernel, lambda_init=lambda_init)
    return pl.pallas_call(
        kern,
        grid=(b, DIFF_HEADS, seq // tq),
        in_specs=[pl.BlockSpec((1, tq, hw), lambda i, h, j: (i, j, h)),
                  pl.BlockSpec((1, seq, hw), lambda i, h, j: (i, 0, h)),
                  pl.BlockSpec((1, seq, hw), lambda i, h, j: (i, 0, h)),
                  small(DIFF_DH), small(DIFF_DH), small(DIFF_DH), small(DIFF_DH), small(hw)],
        out_specs=pl.BlockSpec((1, tq, hw), lambda i, h, j: (i, j, h)),
        out_shape=jax.ShapeDtypeStruct((b, seq, d), BF16),
        scratch_shapes=scratch,
        compiler_params=_params("parallel", "parallel", "parallel"),
        name="diff_attn_stream" if streaming else "diff_attn_exact",
    )(q, k, v, lq1, lk1, lq2, lk2, sub_g)


def _mem_kv_kernel(m_ref, g_ref, w_ref, kg_ref, k_ref, v_ref):
    d = m_ref.shape[-1]
    m = m_ref[0]
    mn = (m * lax.rsqrt(jnp.mean(m * m, axis=-1, keepdims=True) + EPS) * g_ref[...]).astype(BF16)
    kv = _dot(mn, w_ref[...])
    mdh = d // MEM_HEADS
    for h in range(MEM_HEADS):
        seg = kv[:, h * mdh:(h + 1) * mdh]
        segn = seg * lax.rsqrt(jnp.mean(seg * seg, axis=-1, keepdims=True) + EPS) * kg_ref[:, h * mdh:(h + 1) * mdh]
        k_ref[0, :, h * mdh:(h + 1) * mdh] = segn.astype(BF16)
    v_ref[0] = kv[:, d:].astype(BF16)


def mem_kv(mem, g, w_kv, kg):
    b, m, d = mem.shape
    blk = pl.BlockSpec((1, m, d), lambda i: (i, 0, 0))
    return pl.pallas_call(
        _mem_kv_kernel,
        grid=(b,),
        in_specs=[blk, _resident((1, d)), _resident((d, 2 * d)), _resident((1, d))],
        out_specs=[blk, blk],
        out_shape=[jax.ShapeDtypeStruct((b, m, d), BF16)] * 2,
        compiler_params=_params("parallel"),
        name="mem_kv",
    )(mem, g, w_kv, kg)


def _merge_kernel(x_ref, y_ref, do_ref, qm_ref, gt_ref, mk_ref, mv_ref, wglu_ref, wb_ref, wo_ref, o_ref):
    d = x_ref.shape[-1]
    mdh = d // MEM_HEADS
    heads = [slice(h * mdh, (h + 1) * mdh) for h in range(MEM_HEADS)]

    def gated(n, br):
        return gt_ref[0, :, n * d:(n + 1) * d].astype(F32) * _dot(br, wb_ref[n])

    merged = gated(1, do_ref[0])
    qm = qm_ref[0]
    mk = mk_ref[0]
    scores = [_dot_nt(qm[:, sl], mk[:, sl]) for sl in heads]

    z = jax.nn.gelu(y_ref[0].astype(F32))
    s5o = (z * jax.nn.sigmoid(_dot(z.astype(BF16), wglu_ref[...]))).astype(BF16)

    mv = mv_ref[0]
    mem_parts = []
    for s, sl in zip(scores, heads):
        e = jnp.exp(s - jnp.max(s, axis=-1, keepdims=True))
        pr = e * (1.0 / jnp.sum(e, axis=-1, keepdims=True))
        mem_parts.append(_dot(pr.astype(BF16), mv[:, sl]).astype(BF16))
    mo = jnp.concatenate(mem_parts, axis=-1)

    merged = merged + gated(0, s5o) + gated(2, mo)
    o_ref[0] = x_ref[0] + _dot(merged.astype(BF16), wo_ref[...])


def merge(x, y, do, qm, gates, mk, mv, w_glu, w_branch, w_out, tt):
    b, seq, d = x.shape
    m = mk.shape[1]
    tok = lambda width: pl.BlockSpec((1, tt, width), lambda i, j: (i, j, 0))
    memb = pl.BlockSpec((1, m, d), lambda i, j: (i, 0, 0))
    return pl.pallas_call(
        _merge_kernel,
        grid=(b, seq // tt),
        in_specs=[tok(d), tok(d), tok(d), tok(d), tok(3 * d), memb, memb,
                  _resident((d, d)), _resident((3, d, d)), _resident((d, d))],
        out_specs=tok(d),
        out_shape=jax.ShapeDtypeStruct((b, seq, d), F32),
        compiler_params=_params("parallel", "parallel"),
        name="merge",
    )(x, y, do, qm, gates, mk, mv, w_glu, w_branch, w_out)


def _ffn_kernel(x_ref, g_ref, wgu_ref, wd_ref, o_ref):
    dff = wd_ref.shape[0]
    rows = x_ref.shape[1]
    half = rows // 2 if rows % 512 == 0 else rows
    for r0 in range(0, rows, half):
        x = x_ref[0, r0:r0 + half, :]
        hn = (x * lax.rsqrt(jnp.mean(x * x, axis=-1, keepdims=True) + EPS) * g_ref[...]).astype(BF16)
        gate = _dot(hn, wgu_ref[:, :dff])
        up = _dot(hn, wgu_ref[:, dff:])
        act = (jax.nn.silu(gate) * up).astype(BF16)
        o_ref[0, r0:r0 + half, :] = x + _dot(act, wd_ref[...])


def ffn(x, g, w_gate_up, w_down, tt):
    b, seq, d = x.shape
    dff = w_down.shape[0]
    tok = pl.BlockSpec((1, tt, d), lambda i, j: (i, j, 0))
    return pl.pallas_call(
        _ffn_kernel,
        grid=(b, seq // tt),
        in_specs=[tok, _resident((1, d)), _resident((d, 2 * dff)), _resident((dff, d))],
        out_specs=tok,
        out_shape=jax.ShapeDtypeStruct((b, seq, d), F32),
        compiler_params=_params("parallel", "parallel"),
        name="ffn",
    )(x, g, w_gate_up, w_down)


def _s5_mixer(u, s5_ops, perm):
    b, seq, d = u.shape
    nc = seq // CHUNK
    cb = min(S5_CHUNKS_PER_STEP, nc)
    w1, w2, a = s5_ops
    u_tm = jnp.swapaxes(u, 0, 1).reshape(nc, CHUNK, b, d)
    chunk_rows = jax.ShapeDtypeStruct((d // S5_GROUP, nc * b, CHUNK * S5_GROUP), BF16)
    u2 = _chunk_relayout(_to_chunks_kernel, u_tm, perm, chunk_rows, cb, True)
    y2 = s5_chunk(u2, w1, w2, a, b, S5_GROUPS_PER_STEP)
    y_tm = _chunk_relayout(_from_chunks_kernel, y2, perm, jax.ShapeDtypeStruct(u_tm.shape, BF16), cb, False)
    return jnp.swapaxes(y_tm.reshape(seq, b, d), 0, 1)


def _layer(x, mem, p, li, s5_ops, tables, perm):
    b, seq, d = x.shape
    tt = min(DENSE_TOKENS, seq)
    tq = min(ATTN_QUERY_ROWS, seq)
    row = lambda a: a.reshape(1, -1)
    tile_to_d = lambda a: jnp.tile(a, d // a.shape[0]).reshape(1, d)
    head_of_lane = (lax.broadcasted_iota(jnp.int32, (d, LANES), 0) // DIFF_DH
                    == lax.broadcasted_iota(jnp.int32, (d, LANES), 1))
    hsel = jnp.where(head_of_lane, 1.0 / DIFF_DH, 0.0).astype(BF16)
    hexp = head_of_lane.T.astype(BF16)
    cos, sin = tables

    u, q, k, v, qm, gates = proj(x, row(p["norm_mix_g"][li]), p["w_in"][li].astype(BF16), hsel, hexp, row(p["b_gate"][li]),
                                 tile_to_d(p["diff_q_g"][li]), tile_to_d(p["diff_k_g"][li]), tile_to_d(p["mem_q_g"][li]),
                                 cos[:seq], sin[:seq], tt)

    y = _s5_mixer(u, s5_ops, perm)

    lambda_init = 0.8 - 0.6 * math.exp(-0.3 * li)
    attn_args = (q, k, v, row(p["diff_lq1"][li]), row(p["diff_lk1"][li]), row(p["diff_lq2"][li]),
                 row(p["diff_lk2"][li]), row(p["diff_sub_g"][li]))
    score_bound = (1.05 * DIFF_DH ** 0.5 * LOG2E) * jnp.max(jnp.abs(p["diff_q_g"][li])) * jnp.max(jnp.abs(p["diff_k_g"][li]))
    do = lax.cond(score_bound <= SCORE_BOUND,
                  lambda *a: diff_attn(*a, lambda_init, tq, True),
                  lambda *a: diff_attn(*a, lambda_init, tq, False), *attn_args)

    mk, mv = mem_kv(mem, row(p["mem_norm_g"][li]), p["w_mem_kv"][li].astype(BF16), tile_to_d(p["mem_k_g"][li]))
    x1 = merge(x, y, do, qm, gates, mk, mv, p["s5_w_glu"][li].astype(BF16), p["w_branch"][li].astype(BF16),
               p["w_out"][li].astype(BF16), tt)
    return ffn(x1, row(p["ffn_norm_g"][li]), p["w_gate_up"][li].astype(BF16), p["w_down"][li].astype(BF16), tt)


def kernel(x_prompt, x_sample, mem_prompt, mem_sample, norm_mix_g, w_in, b_gate, s5_lam_re, s5_lam_im, s5_log_dt, s5_b_re, s5_b_im, s5_c_re, s5_c_im, s5_d, s5_w_glu, diff_q_g, diff_k_g, diff_lq1, diff_lk1, diff_lq2, diff_lk2, diff_sub_g, mem_norm_g, w_mem_kv, mem_q_g, mem_k_g, w_branch, w_out, ffn_norm_g, w_gate_up, w_down):
    p = dict(norm_mix_g=norm_mix_g, w_in=w_in, b_gate=b_gate, s5_w_glu=s5_w_glu, diff_q_g=diff_q_g, diff_k_g=diff_k_g,
             diff_lq1=diff_lq1, diff_lk1=diff_lk1, diff_lq2=diff_lq2, diff_lk2=diff_lk2, diff_sub_g=diff_sub_g,
             mem_norm_g=mem_norm_g, w_mem_kv=w_mem_kv, mem_q_g=mem_q_g, mem_k_g=mem_k_g, w_branch=w_branch,
             w_out=w_out, ffn_norm_g=ffn_norm_g, w_gate_up=w_gate_up, w_down=w_down)
    tables = rope_tables(max(x_prompt.shape[1], x_sample.shape[1]))
    perm = jnp.asarray(_chunk_permutation(), BF16)
    y_prompt, y_sample = x_prompt, x_sample
    for li in range(norm_mix_g.shape[0]):
        s5_ops = s5_prep(s5_lam_re[li], s5_lam_im[li], s5_log_dt[li], s5_b_re[li], s5_b_im[li],
                         s5_c_re[li], s5_c_im[li], s5_d[li])
        y_prompt = _layer(y_prompt, mem_prompt, p, li, s5_ops, tables, perm)
        y_sample = _layer(y_sample, mem_sample, p, li, s5_ops, tables, perm)
    return (y_prompt, y_sample)
```
